```python
import math
import jax, jax.numpy as jnp
from jax import lax
import numpy as np

D_MODEL = 1024
BATCH = 8
SEQ = 4096
DEPTH = 1

D_MIX = D_MODEL
D_MLSTM = D_MIX // 2
D_DIFF = D_MIX - D_MLSTM
M_HEADS = 4
M_HEAD_DIM = D_MLSTM // M_HEADS
CHUNK = 64
CONV_K = 5
N_GATE_COLS = 4 * M_HEADS
A_HEADS = 4
A_HEAD_DIM = D_DIFF // (2 * A_HEADS)
ROPE_THETA = 500000.0
ROT_DIM = A_HEAD_DIM // 4
Q_BLOCK = 128
N_EXPERTS = 16
CAP_FACTOR = 2
D_FF_E = 2 * D_MODEL
EPS = 1e-6
D_IN = 4 * D_MLSTM + N_GATE_COLS + 3 * D_DIFF

kernel_name = 'hybrid_mlstm_diffattn_ec_moe_block'


def rmsnorm(x, g):
    x32 = x.astype(jnp.float32)
    y = x32 * lax.rsqrt(jnp.mean(x32 * x32, axis=-1, keepdims=True) + EPS)
    return (y * g.astype(jnp.float32)).astype(x.dtype)


def modulate(h, shift, scale):
    return h * (1 + scale[:, None, :]) + shift[:, None, :]


def split_cols(u, widths):
    outs = []
    off = 0
    for w in widths:
        outs.append(u[..., off:off + w])
        off += w
    return outs


def conv_centered(u, w):
    S = u.shape[1]
    pad = CONV_K // 2
    up = jnp.pad(u, ((0, 0), (pad, pad), (0, 0)))
    out = up[:, 0:S] * w[0]
    for j in range(1, CONV_K):
        out = out + up[:, j:j + S] * w[j]
    return out


def rope_partial(t, pos):
    inv_freq = ROPE_THETA ** (-jnp.arange(0, ROT_DIM, 2, dtype=jnp.float32) / ROT_DIM)
    ang = pos.astype(jnp.float32)[:, :, None] * inv_freq
    cos = jnp.cos(ang)[:, :, None, None, :].astype(t.dtype)
    sin = jnp.sin(ang)[:, :, None, None, :].astype(t.dtype)
    half = ROT_DIM // 2
    t1 = t[..., :half]
    t2 = t[..., half:ROT_DIM]
    rot = jnp.concatenate([t1 * cos - t2 * sin, t2 * cos + t1 * sin], axis=-1)
    return jnp.concatenate([rot, t[..., ROT_DIM:]], axis=-1)


def mlstm_chunkwise(q, k, v, i_pre, f_pre):
    B, H, S, d = q.shape
    NC = S // CHUNK
    q = q.reshape(B, H, NC, CHUNK, d)
    k = k.reshape(B, H, NC, CHUNK, d)
    v = v.reshape(B, H, NC, CHUNK, d)
    ig = i_pre.reshape(B, H, NC, CHUNK)
    a = jnp.cumsum(jax.nn.log_sigmoid(f_pre).reshape(B, H, NC, CHUNK), axis=-1)
    g = a[..., -1]
    w_log = g[..., None] - a + ig
    m_loc = jnp.max(w_log, axis=-1)
    w = jnp.exp(w_log - m_loc[..., None])
    C_loc = jnp.einsum('bhcl,bhclk,bhclv->bhckv', w, k, v)
    n_loc = jnp.einsum('bhcl,bhclk->bhck', w, k)

    def step(carry, inp):
        C, n, m = carry
        Cl, nl, ml, gc = inp
        m_new = jnp.maximum(gc + m, ml)
        s_prev = jnp.exp(gc + m - m_new)
        s_loc = jnp.exp(ml - m_new)
        C_new = s_prev[..., None, None] * C + s_loc[..., None, None] * Cl
        n_new = s_prev[..., None] * n + s_loc[..., None] * nl
        return (C_new, n_new, m_new), (C, n, m)

    init = (jnp.zeros((B, H, d, d), jnp.float32), jnp.zeros((B, H, d), jnp.float32),
            jnp.zeros((B, H), jnp.float32))
    xs = (jnp.moveaxis(C_loc, 2, 0), jnp.moveaxis(n_loc, 2, 0),
          jnp.moveaxis(m_loc, 2, 0), jnp.moveaxis(g, 2, 0))
    _, (C_prev, n_prev, m_prev) = lax.scan(step, init, xs)
    C_prev = jnp.moveaxis(C_prev, 0, 2)
    n_prev = jnp.moveaxis(n_prev, 0, 2)
    m_prev = jnp.moveaxis(m_prev, 0, 2)
    D = a[..., :, None] - a[..., None, :] + ig[..., None, :]
    tril = jnp.tril(jnp.ones((CHUNK, CHUNK), dtype=bool))
    D = jnp.where(tril, D, -jnp.inf)
    inter_log = a + m_prev[..., None]
    m_j = jnp.maximum(inter_log, jnp.max(D, axis=-1))
    inter_w = jnp.exp(inter_log - m_j)
    s = jnp.einsum('bhcjd,bhcsd->bhcjs', q, k) * jnp.exp(D - m_j[..., None])
    num = (jnp.einsum('bhcjs,bhcsv->bhcjv', s, v)
           + inter_w[..., None] * jnp.einsum('bhcjk,bhckv->bhcjv', q, C_prev))
    den = jnp.sum(s, axis=-1) + inter_w * jnp.einsum('bhcjk,bhck->bhcj', q, n_prev)
    den = jnp.maximum(jnp.abs(den), jnp.exp(-m_j))
    return (num / den[..., None]).reshape(B, H, S, d)


def hybrid_layer(x, c, positions, layer_idx, norm1_g, norm2_g, w_ada, b_ada, w_in,
                 mlstm_conv_w, mlstm_gate_b, mlstm_norm_g, diff_qk_g, diff_lambda,
                 diff_subln_g, w_out, w_router, w_gate_e, w_up_e, w_down_e):
    B, S, _ = x.shape
    dt = x.dtype
    mod = jax.nn.silu(c) @ w_ada + b_ada
    shift1, scale1, gate1, shift2, scale2, gate2 = jnp.split(mod, 6, axis=-1)

    h = modulate(rmsnorm(x, norm1_g), shift1, scale1)
    proj = h @ w_in
    qm, km, vm, om, gm, qa, ka, va = split_cols(
        proj, [D_MLSTM, D_MLSTM, D_MLSTM, D_MLSTM, N_GATE_COLS, D_DIFF, D_DIFF, D_DIFF])

    qk = jax.nn.silu(conv_centered(jnp.concatenate([qm, km], axis=-1), mlstm_conv_w))
    qm, km = qk[..., :D_MLSTM], qk[..., D_MLSTM:]

    def to_heads(t):
        return t.reshape(B, S, M_HEADS, M_HEAD_DIM).transpose(0, 2, 1, 3).astype(jnp.float32)

    q_h = to_heads(qm)
    k_h = to_heads(km) * (M_HEAD_DIM ** -0.5)
    v_h = to_heads(vm)
    gates = (gm + mlstm_gate_b).astype(jnp.float32).reshape(B, S, 4, M_HEADS).transpose(0, 2, 3, 1)
    i_f, f_f, i_b, f_b = gates[:, 0], gates[:, 1], gates[:, 2], gates[:, 3]
    h_fwd = mlstm_chunkwise(q_h, k_h, v_h, i_f, f_f)
    fl = lambda t: jnp.flip(t, axis=2)
    h_bwd = fl(mlstm_chunkwise(fl(q_h), fl(k_h), fl(v_h), fl(i_b), fl(f_b)))
    h_m = (h_fwd + h_bwd).transpose(0, 2, 1, 3)
    h_m = rmsnorm(h_m, mlstm_norm_g.reshape(M_HEADS, M_HEAD_DIM)).astype(dt)
    y_m = jax.nn.sigmoid(om) * h_m.reshape(B, S, D_MLSTM)

    lam_init = 0.8 - 0.6 * math.exp(-0.3 * layer_idx)
    lam_p = diff_lambda.astype(jnp.float32)
    lam = (jnp.exp(jnp.sum(lam_p[0] * lam_p[1])) - jnp.exp(jnp.sum(lam_p[2] * lam_p[3]))
           + lam_init)
    qa = rope_partial(rmsnorm(qa.reshape(B, S, A_HEADS, 2, A_HEAD_DIM), diff_qk_g[0]), positions)
    ka = rope_partial(rmsnorm(ka.reshape(B, S, A_HEADS, 2, A_HEAD_DIM), diff_qk_g[1]), positions)
    q_t = qa.transpose(0, 2, 3, 1, 4)
    k_t = ka.transpose(0, 2, 3, 1, 4)
    v_t = va.reshape(B, S, A_HEADS, 2 * A_HEAD_DIM).transpose(0, 2, 1, 3)
    NB = S // Q_BLOCK
    q_blocks = q_t.reshape(B, A_HEADS, 2, NB, Q_BLOCK, A_HEAD_DIM).transpose(3, 0, 1, 2, 4, 5)
    scale = A_HEAD_DIM ** -0.5

    def attend(q_blk):
        s = jnp.einsum('bhpqd,bhpkd->bhpqk', q_blk, k_t).astype(jnp.float32) * scale
        p = jax.nn.softmax(s, axis=-1)
        a_map = p[:, :, 0] - lam * p[:, :, 1]
        return jnp.einsum('bhqk,bhkv->bhqv', a_map.astype(v_t.dtype), v_t)

    o = lax.map(attend, q_blocks)
    o = o.transpose(1, 0, 3, 2, 4).reshape(B, S, A_HEADS, 2 * A_HEAD_DIM)
    o = rmsnorm(o, diff_subln_g) * (1.0 - lam_init)
    y_a = o.reshape(B, S, D_DIFF).astype(dt)

    mix = jnp.concatenate([y_m, y_a], axis=-1) @ w_out
    x = x + gate1[:, None, :] * mix

    h2 = modulate(rmsnorm(x, norm2_g), shift2, scale2)
    aff = jax.nn.softmax(jnp.einsum('bsd,de->bse', h2, w_router).astype(jnp.float32), axis=-1)
    cap = CAP_FACTOR * S // N_EXPERTS
    vals, idx = lax.top_k(aff.transpose(0, 2, 1), cap)
    xe = jax.vmap(lambda hb, ib: hb[ib])(h2, idx)
    g_e = jnp.einsum('becd,edf->becf', xe, w_gate_e)
    u_e = jnp.einsum('becd,edf->becf', xe, w_up_e)
    out_e = jnp.einsum('becf,efd->becd', jax.nn.silu(g_e) * u_e, w_down_e)
    out_e = out_e * vals[..., None].astype(dt)
    moe = jnp.zeros_like(h2).at[jnp.arange(B)[:, None, None], idx].add(out_e)
    return x + gate2[:, None, :] * moe


def setup_inputs(seed: int = 0) -> dict:
    key = jax.random.key(seed)
    ks = jax.random.split(key, 24)
    D = D_MODEL
    nrm = lambda k, shape, s: jax.random.normal(k, shape, jnp.float32) * s
    x = jax.random.normal(ks[0], (BATCH, SEQ, D), jnp.float32)
    c = jax.random.normal(ks[1], (BATCH, D), jnp.float32)
    offs = jax.random.randint(ks[2], (BATCH, 1), 0, 1024, dtype=jnp.int32)
    positions = offs + jnp.arange(SEQ, dtype=jnp.int32)[None, :]
    norm1_g = 1.0 + nrm(ks[3], (DEPTH, D), 0.02)
    norm2_g = 1.0 + nrm(ks[4], (DEPTH, D), 0.02)
    w_ada = nrm(ks[5], (DEPTH, D, 6 * D), 0.5 * D ** -0.5)
    b_ada = nrm(ks[6], (DEPTH, 6 * D), 0.02)
    w_in = nrm(ks[7], (DEPTH, D, D_IN), D ** -0.5)
    mlstm_conv_w = nrm(ks[8], (DEPTH, CONV_K, 2 * D_MLSTM), CONV_K ** -0.5)
    gk = jax.random.split(ks[9], 4)
    f_init = jnp.linspace(3.0, 6.0, M_HEADS, dtype=jnp.float32)
    mlstm_gate_b = jnp.concatenate([
        nrm(gk[0], (DEPTH, M_HEADS), 0.1),
        f_init + nrm(gk[1], (DEPTH, M_HEADS), 0.1),
        nrm(gk[2], (DEPTH, M_HEADS), 0.1),
        f_init + nrm(gk[3], (DEPTH, M_HEADS), 0.1)], axis=-1)
    mlstm_norm_g = 1.0 + nrm(ks[10], (DEPTH, D_MLSTM), 0.02)
    diff_qk_g = 1.0 + nrm(ks[11], (DEPTH, 2, A_HEAD_DIM), 0.02)
    diff_lambda = nrm(ks[12], (DEPTH, 4, A_HEAD_DIM), 0.1)
    diff_subln_g = 1.0 + nrm(ks[13], (DEPTH, 2 * A_HEAD_DIM), 0.02)
    w_out = nrm(ks[14], (DEPTH, D_MIX, D), D_MIX ** -0.5)
    w_router = nrm(ks[15], (DEPTH, D, N_EXPERTS), D ** -0.5)
    w_gate_e = nrm(ks[16], (DEPTH, N_EXPERTS, D, D_FF_E), D ** -0.5)
    w_up_e = nrm(ks[17], (DEPTH, N_EXPERTS, D, D_FF_E), D ** -0.5)
    w_down_e = nrm(ks[18], (DEPTH, N_EXPERTS, D_FF_E, D), D_FF_E ** -0.5)
    return {'x': x, 'c': c, 'positions': positions, 'norm1_g': norm1_g, 'norm2_g': norm2_g,
            'w_ada': w_ada, 'b_ada': b_ada, 'w_in': w_in, 'mlstm_conv_w': mlstm_conv_w,
            'mlstm_gate_b': mlstm_gate_b, 'mlstm_norm_g': mlstm_norm_g, 'diff_qk_g': diff_qk_g,
            'diff_lambda': diff_lambda, 'diff_subln_g': diff_subln_g, 'w_out': w_out,
            'w_router': w_router, 'w_gate_e': w_gate_e, 'w_up_e': w_up_e, 'w_down_e': w_down_e}


def reference(x, c, positions, norm1_g, norm2_g, w_ada, b_ada, w_in, mlstm_conv_w,
              mlstm_gate_b, mlstm_norm_g, diff_qk_g, diff_lambda, diff_subln_g, w_out,
              w_router, w_gate_e, w_up_e, w_down_e):
    for l in range(DEPTH):
        x = hybrid_layer(x, c, positions, l, norm1_g[l], norm2_g[l], w_ada[l], b_ada[l],
                         w_in[l], mlstm_conv_w[l], mlstm_gate_b[l], mlstm_norm_g[l],
                         diff_qk_g[l], diff_lambda[l], diff_subln_g[l], w_out[l],
                         w_router[l], w_gate_e[l], w_up_e[l], w_down_e[l])
    return x
```

```python
import functools
import math

import numpy as np
import jax
import jax.numpy as jnp
from jax import lax
from jax.experimental import pallas as pl
from jax.experimental.pallas import tpu as pltpu

F32 = jnp.float32
BF16 = jnp.bfloat16
I32 = jnp.int32

M_HEADS = 4
A_HEADS = 4
CONV_K = 5
ROPE_THETA = 500000.0
EPS = 1e-6
CAP_FACTOR = 2

MLSTM_CHUNK = 256
ROW_TILE = 512
ATT_TQ = 256
ATT_TK = 512
TOK_TILE = 256
BF16_ROWS = 16
VMEM_LIMIT = 56 * 1024 * 1024


def _cparams(sem):
    return pltpu.CompilerParams(dimension_semantics=sem, vmem_limit_bytes=VMEM_LIMIT)


def _sigmoid(v):
    return 1.0 / (1.0 + jnp.exp(-v))


def _rms(v):
    return v * lax.rsqrt(jnp.mean(v * v, axis=-1, keepdims=True) + EPS)


def _ada_body(c_ref, w_ref, b_ref, o_ref):
    c = c_ref[...]
    s = (c * _sigmoid(c)).astype(BF16)
    o_ref[...] = jnp.dot(s, w_ref[...].astype(BF16), preferred_element_type=F32) + b_ref[...]


def _ada(c, w_ada, b_ada):
    B, D = c.shape
    N = w_ada.shape[1]
    tn = D
    return pl.pallas_call(
        _ada_body,
        grid=(N // tn,),
        in_specs=[pl.BlockSpec((B, D), lambda j: (0, 0)),
                  pl.BlockSpec((D, tn), lambda j: (0, j)),
                  pl.BlockSpec((1, tn), lambda j: (0, j))],
        out_specs=pl.BlockSpec((B, tn), lambda j: (0, j)),
        out_shape=jax.ShapeDtypeStruct((B, N), F32),
        compiler_params=_cparams(("arbitrary",)),
        name="ada",
    )(c, w_ada, b_ada.reshape(1, N))


def _inproj_body(x_ref, mod_ref, g_ref, wm_ref, wa_ref, wg_ref, bg_ref, pm_ref, pa_ref, gt_ref):
    x = x_ref[0]
    h = _rms(x) * g_ref[...]
    h = (h * (1.0 + mod_ref[0, 1:2, :]) + mod_ref[0, 0:1, :]).astype(BF16)
    pm_ref[0] = jnp.dot(h, wm_ref[...], preferred_element_type=F32).astype(BF16)
    pa_ref[0] = jnp.dot(h, wa_ref[...], preferred_element_type=F32).astype(BF16)
    gt_ref[0] = jnp.dot(h, wg_ref[...], preferred_element_type=F32) + bg_ref[...]


def _inproj(x, mod, g1, wm, wa, wg, bg):
    B, S, D = x.shape
    tm = min(ROW_TILE, S)
    nm, na, ng = wm.shape[1], wa.shape[1], wg.shape[1]
    return pl.pallas_call(
        _inproj_body,
        grid=(B, S // tm),
        in_specs=[pl.BlockSpec((1, tm, D), lambda b, i: (b, i, 0)),
                  pl.BlockSpec((1, 6, D), lambda b, i: (b, 0, 0)),
                  pl.BlockSpec((1, D), lambda b, i: (0, 0)),
                  pl.BlockSpec((D, nm), lambda b, i: (0, 0)),
                  pl.BlockSpec((D, na), lambda b, i: (0, 0)),
                  pl.BlockSpec((D, ng), lambda b, i: (0, 0)),
                  pl.BlockSpec((1, ng), lambda b, i: (0, 0))],
        out_specs=[pl.BlockSpec((1, tm, nm), lambda b, i: (b, i, 0)),
                   pl.BlockSpec((1, tm, na), lambda b, i: (b, i, 0)),
                   pl.BlockSpec((1, tm, ng), lambda b, i: (b, i, 0))],
        out_shape=[jax.ShapeDtypeStruct((B, S, nm), BF16),
                   jax.ShapeDtypeStruct((B, S, na), BF16),
                   jax.ShapeDtypeStruct((B, S, ng), F32)],
        compiler_params=_cparams(("arbitrary", "arbitrary")),
        name="inproj",
    )(x, mod, g1, wm, wa, wg, bg)


def _log_sigmoid(v):
    return jnp.minimum(v, 0.0) - jnp.log(1.0 + jnp.exp(-jnp.abs(v)))


def _mlstm_body(q_ref, k_ref, v_ref, o_ref, cwq_ref, cwk_ref, gt_ref, ng_ref, y_ref,
                xp_s, q_s, kt_s, va_s, hf_s, a_s, *, S, L, dh):
    hd = pl.program_id(1)
    nc = S // L
    win = L + 16

    def conv(src_ref, cw_ref, post):
        xp_s[0:8, :] = jnp.zeros((8, dh), F32)
        xp_s[S + 8:S + 16, :] = jnp.zeros((8, dh), F32)

        def fill(c, _):
            base = pl.multiple_of(c * L, L)
            xp_s[pl.ds(pl.multiple_of(base + 8, 8), L), :] = src_ref[0, pl.ds(base, L), :].astype(F32)
            return 0
        lax.fori_loop(0, nc, fill, 0)

        def body(c, _):
            base = pl.multiple_of(c * L, L)
            w = xp_s[pl.ds(base, win), :]
            acc = None
            for j in range(CONV_K):
                sh = (CONV_K // 2 - j) % win
                r = w if sh == 0 else pltpu.roll(w, sh, 0)
                term = r[8:8 + L, :] * cw_ref[j:j + 1, :]
                acc = term if acc is None else acc + term
            post(base, acc * _sigmoid(acc))
            return 0
        lax.fori_loop(0, nc, body, 0)

    def post_q(base, a):
        q_s[pl.ds(base, L), :] = a.astype(BF16)

    def post_k(base, a):
        kt_s[:, pl.ds(base, L)] = (a * (dh ** -0.5)).T.astype(BF16)

    conv(q_ref, cwq_ref, post_q)
    conv(k_ref, cwk_ref, post_k)

    ones_col = (lax.broadcasted_iota(I32, (L, dh), 1) == 0).astype(BF16)

    def fill_v(c, _):
        base = pl.multiple_of(c * L, L)
        va_s[pl.ds(base, L), 0:dh] = v_ref[0, pl.ds(base, L), :]
        va_s[pl.ds(base, L), dh:2 * dh] = ones_col
        return 0
    lax.fori_loop(0, nc, fill_v, 0)

    ri = lax.broadcasted_iota(I32, (L, L), 0)
    ci = lax.broadcasted_iota(I32, (L, L), 1)

    def scan(reverse, gi, gf, emit):
        logf_all = _log_sigmoid(gt_ref[0, gf])
        tri = ((ri >= ci) if reverse else (ri <= ci)).astype(F32)
        a_s[...] = jnp.dot(logf_all, tri, preferred_element_type=F32, precision=lax.Precision.HIGHEST)
        valid = (ci >= ri) if reverse else (ci <= ri)

        def body(step, carry):
            caug, m = carry
            c = (nc - 1 - step) if reverse else step
            base = pl.multiple_of(c * L, L)
            i_row = gt_ref[0, gi, pl.ds(c, 1), :]
            logf = _log_sigmoid(gt_ref[0, gf, pl.ds(c, 1), :])
            a_row = a_s[pl.ds(c, 1), :]
            a_col = jnp.sum(jnp.where(valid, logf, 0.0), axis=1, keepdims=True)
            g = jnp.sum(logf, axis=1, keepdims=True)
            dmat = jnp.where(valid, a_col - a_row + i_row, -jnp.inf)
            inter = a_col + m
            mj = jnp.maximum(inter, jnp.max(dmat, axis=1, keepdims=True))
            q_c = q_s[pl.ds(base, L), :]
            kt_c = kt_s[:, pl.ds(base, L)]
            va_c = va_s[pl.ds(base, L), :]
            qk = jnp.dot(q_c, kt_c, preferred_element_type=F32)
            p = (qk * jnp.exp(dmat - mj)).astype(BF16)
            nd = (jnp.dot(p, va_c, preferred_element_type=F32)
                  + jnp.exp(inter - mj) * jnp.dot(q_c, caug.astype(BF16), preferred_element_type=F32))
            den = jnp.maximum(jnp.abs(nd[:, dh:dh + 1]), jnp.exp(-mj))
            emit(base, nd[:, 0:dh] * (1.0 / den))
            wlog = g - a_row + i_row
            mloc = jnp.max(wlog, axis=1, keepdims=True)
            kw = (kt_c.astype(F32) * jnp.exp(wlog - mloc)).astype(BF16)
            cloc = jnp.dot(kw, va_c, preferred_element_type=F32)
            m_new = jnp.maximum(g + m, mloc)
            caug = jnp.exp(g + m - m_new) * caug + jnp.exp(mloc - m_new) * cloc
            return caug, m_new

        lax.fori_loop(0, nc, body, (jnp.zeros((dh, 2 * dh), F32), jnp.zeros((1, 1), F32)))

    def emit_f(base, h):
        hf_s[pl.ds(base, L), :] = h

    def emit_b(base, h):
        hn = _rms(hf_s[pl.ds(base, L), :] + h) * ng_ref[...]
        og = o_ref[0, pl.ds(base, L), :].astype(F32)
        y_ref[0, pl.ds(base, L), :] = (_sigmoid(og) * hn).astype(BF16)

    scan(False, hd, M_HEADS + hd, emit_f)
    scan(True, 2 * M_HEADS + hd, 3 * M_HEADS + hd, emit_b)


def _mlstm(pm, conv_w, gt, norm_g):
    B, S, _ = pm.shape
    dh = norm_g.shape[1] // M_HEADS
    L = min(MLSTM_CHUNK, S)
    nc = S // L
    H = M_HEADS
    body = functools.partial(_mlstm_body, S=S, L=L, dh=dh)
    return pl.pallas_call(
        body,
        grid=(B, H),
        in_specs=[pl.BlockSpec((1, S, dh), lambda b, h: (b, 0, h)),
                  pl.BlockSpec((1, S, dh), lambda b, h: (b, 0, H + h)),
                  pl.BlockSpec((1, S, dh), lambda b, h: (b, 0, 2 * H + h)),
                  pl.BlockSpec((1, S, dh), lambda b, h: (b, 0, 3 * H + h)),
                  pl.BlockSpec((CONV_K, dh), lambda b, h: (0, h)),
                  pl.BlockSpec((CONV_K, dh), lambda b, h: (0, H + h)),
                  pl.BlockSpec((1, 4 * H, nc, L), lambda b, h: (b, 0, 0, 0)),
                  pl.BlockSpec((1, dh), lambda b, h: (0, h))],
        out_specs=pl.BlockSpec((1, S, dh), lambda b, h: (b, 0, h)),
        out_shape=jax.ShapeDtypeStruct((B, S, H * dh), BF16),
        scratch_shapes=[pltpu.VMEM((S + 16, dh), F32),
                        pltpu.VMEM((S, dh), BF16),
                        pltpu.VMEM((dh, S), BF16),
                        pltpu.VMEM((S, 2 * dh), BF16),
                        pltpu.VMEM((S, dh), F32),
                        pltpu.VMEM((nc, L), F32)],
        compiler_params=_cparams(("arbitrary", "arbitrary")),
        name="mlstm",
    )(pm, pm, pm, pm, conv_w, conv_w, gt, norm_g)


def _qkprep_body(q_ref, k_ref, pos_ref, invf_ref, s1_ref, s2_ref, gm_ref, gq_ref, gk_ref,
                 qo_ref, ko_ref, *, groups, q_scale):
    ang = pos_ref[0].astype(F32) * invf_ref[...]
    cs = jnp.cos(ang)
    sn = jnp.sin(ang)
    s1 = s1_ref[...]
    s2 = s2_ref[...]
    gm = gm_ref[...]
    lanes = gm.shape[0]

    def prep(src_ref, g_ref, dst_ref, scale):
        for j in range(groups):
            x = src_ref[0, :, j * lanes:(j + 1) * lanes].astype(F32)
            ms = jnp.dot((x * x).astype(BF16), gm, preferred_element_type=F32)
            y = x * lax.rsqrt(ms + EPS) * g_ref[...]
            half = pltpu.roll(y, lanes - 8, 1) * s1 + pltpu.roll(y, 8, 1) * s2
            r = y * cs + half * sn
            dst_ref[0, :, j * lanes:(j + 1) * lanes] = (r * scale).astype(BF16)

    prep(q_ref, gq_ref, qo_ref, q_scale)
    prep(k_ref, gk_ref, ko_ref, 1.0)


def _qkprep(pa, positions, qk_g):
    B, S, _ = pa.shape
    hd = qk_g.shape[1]
    rot = hd // 4
    half = rot // 2
    lanes = 128
    W = A_HEADS * 2 * hd
    tm = min(ROW_TILE, S)
    d = np.arange(lanes) % hd
    inv_freq = (ROPE_THETA ** (-np.arange(0, rot, 2, dtype=np.float32) / rot)).astype(np.float32)
    invf = np.where(d < rot, inv_freq[d % half], 0.0).astype(np.float32)[None, :]
    s1 = np.where(d < half, -1.0, 0.0).astype(np.float32)[None, :]
    s2 = np.where((d >= half) & (d < rot), 1.0, 0.0).astype(np.float32)[None, :]
    gm = (np.arange(lanes)[:, None] // hd == np.arange(lanes)[None, :] // hd).astype(np.float32) / hd
    gq = jnp.tile(qk_g[0], lanes // hd)[None, :]
    gk = jnp.tile(qk_g[1], lanes // hd)[None, :]
    q_scale = (hd ** -0.5) * math.log2(math.e)
    body = functools.partial(_qkprep_body, groups=W // lanes, q_scale=q_scale)
    small = lambda shape: pl.BlockSpec(shape, lambda b, i: (0, 0))
    return pl.pallas_call(
        body,
        grid=(B, S // tm),
        in_specs=[pl.BlockSpec((1, tm, W), lambda b, i: (b, i, 0)),
                  pl.BlockSpec((1, tm, W), lambda b, i: (b, i, 1)),
                  pl.BlockSpec((1, tm, 1), lambda b, i: (b, i, 0)),
                  small((1, lanes)), small((1, lanes)), small((1, lanes)), small((lanes, lanes)),
                  small((1, lanes)), small((1, lanes))],
        out_specs=[pl.BlockSpec((1, tm, W), lambda b, i: (b, i, 0)),
                   pl.BlockSpec((1, tm, W), lambda b, i: (b, i, 0))],
        out_shape=[jax.ShapeDtypeStruct((B, S, W), BF16), jax.ShapeDtypeStruct((B, S, W), BF16)],
        compiler_params=_cparams(("arbitrary", "arbitrary")),
        name="qkprep",
    )(pa, pa, positions.reshape(B, S, 1), jnp.asarray(invf), jnp.asarray(s1), jnp.asarray(s2),
      jnp.asarray(gm, dtype=BF16), gq, gk)


def _attn_body(q_ref, k_ref, v_ref, lam_ref, sg_ref, y_ref, va_s, *, S, tq, tk, dv, lam_init):
    nk = S // tk

    @pl.when(pl.program_id(2) == 0)
    def _():
        ones_col = (lax.broadcasted_iota(I32, (tk, dv), 1) == 0).astype(BF16)
        for kb in range(nk):
            va_s[kb * tk:(kb + 1) * tk, 0:dv] = v_ref[0, kb * tk:(kb + 1) * tk, :]
            va_s[kb * tk:(kb + 1) * tk, dv:2 * dv] = ones_col

    lp = lam_ref[...]
    lam = (jnp.exp(jnp.sum(lp[0:1, :] * lp[1:2, :], axis=1, keepdims=True))
           - jnp.exp(jnp.sum(lp[2:3, :] * lp[3:4, :], axis=1, keepdims=True)) + lam_init)

    q = q_ref[0]
    lane = lax.broadcasted_iota(I32, q.shape, 1)
    hd = q.shape[1] // 2

    def softmax_pv(qm):
        def body(kb, carry):
            m, acc = carry
            off = pl.multiple_of(kb * tk, tk)
            s = lax.dot_general(qm, k_ref[0, pl.ds(off, tk), :], (((1,), (1,)), ((), ())),
                                preferred_element_type=F32)
            m_new = jnp.maximum(m, jnp.max(s, axis=1, keepdims=True))
            p = jnp.exp2(s - m_new).astype(BF16)
            acc = jnp.exp2(m - m_new) * acc + jnp.dot(p, va_s[pl.ds(off, tk), :], preferred_element_type=F32)
            return m_new, acc
        _, acc = lax.fori_loop(0, nk, body, (jnp.full((tq, 1), -jnp.inf, F32), jnp.zeros((tq, 2 * dv), F32)))
        return acc[:, 0:dv] * (1.0 / acc[:, dv:dv + 1])

    o = softmax_pv(jnp.where(lane < hd, q, jnp.zeros_like(q))) \
        - lam * softmax_pv(jnp.where(lane >= hd, q, jnp.zeros_like(q)))
    y_ref[0] = (_rms(o) * sg_ref[...] * (1.0 - lam_init)).astype(BF16)


def _attn(qn, kn, pa, lam_p, subln_g, layer_idx):
    B, S, W = qn.shape
    dv = W // A_HEADS
    tq = min(ATT_TQ, S)
    tk = min(ATT_TK, S)
    lam_init = 0.8 - 0.6 * math.exp(-0.3 * layer_idx)
    vcol0 = 2 * W // dv
    body = functools.partial(_attn_body, S=S, tq=tq, tk=tk, dv=dv, lam_init=lam_init)
    return pl.pallas_call(
        body,
        grid=(B, A_HEADS, S // tq),
        in_specs=[pl.BlockSpec((1, tq, dv), lambda b, h, i: (b, i, h)),
                  pl.BlockSpec((1, S, dv), lambda b, h, i: (b, 0, h)),
                  pl.BlockSpec((1, S, dv), lambda b, h, i: (b, 0, vcol0 + h)),
                  pl.BlockSpec(lam_p.shape, lambda b, h, i: (0, 0)),
                  pl.BlockSpec((1, dv), lambda b, h, i: (0, 0))],
        out_specs=pl.BlockSpec((1, tq, dv), lambda b, h, i: (b, i, h)),
        out_shape=jax.ShapeDtypeStruct((B, S, W), BF16),
        scratch_shapes=[pltpu.VMEM((S, 2 * dv), BF16)],
        compiler_params=_cparams(("arbitrary", "arbitrary", "arbitrary")),
        name="attn",
    )(qn, kn, pa, lam_p, subln_g.reshape(1, dv))


def _outproj_body(ym_ref, ya_ref, x_ref, mod_ref, wom_ref, woa_ref, g2_ref, wrh_ref, wrl_ref,
                  x1_ref, h2_ref, aff_ref):
    mix = (jnp.dot(ym_ref[0], wom_ref[...], preferred_element_type=F32)
           + jnp.dot(ya_ref[0], woa_ref[...], preferred_element_type=F32))
    x1 = x_ref[0] + mod_ref[0, 2:3, :] * mix
    x1_ref[0] = x1
    h2 = _rms(x1) * g2_ref[...] * (1.0 + mod_ref[0, 4:5, :]) + mod_ref[0, 3:4, :]
    hh = h2.astype(BF16)
    h2_ref[0] = hh
    hl = (h2 - hh.astype(F32)).astype(BF16)
    logits = (jnp.dot(hh, wrh_ref[...], preferred_element_type=F32)
              + jnp.dot(hl, wrh_ref[...], preferred_element_type=F32)
              + jnp.dot(hh, wrl_ref[...], preferred_element_type=F32))
    ex = jnp.exp(logits - jnp.max(logits, axis=1, keepdims=True))
    aff_ref[0] = ex * (1.0 / jnp.sum(ex, axis=1, keepdims=True))


def _outproj(ym, ya, x, mod, wom, woa, g2, wrh, wrl):
    B, S, D = x.shape
    tm = min(ROW_TILE, S)
    wm, wa, E = ym.shape[2], ya.shape[2], wrh.shape[1]
    full = lambda shape: pl.BlockSpec(shape, lambda b, i: (0, 0))
    return pl.pallas_call(
        _outproj_body,
        grid=(B, S // tm),
        in_specs=[pl.BlockSpec((1, tm, wm), lambda b, i: (b, i, 0)),
                  pl.BlockSpec((1, tm, wa), lambda b, i: (b, i, 0)),
                  pl.BlockSpec((1, tm, D), lambda b, i: (b, i, 0)),
                  pl.BlockSpec((1, 6, D), lambda b, i: (b, 0, 0)),
                  full((wm, D)), full((wa, D)), full((1, D)), full((D, E)), full((D, E))],
        out_specs=[pl.BlockSpec((1, tm, D), lambda b, i: (b, i, 0)),
                   pl.BlockSpec((1, tm, D), lambda b, i: (b, i, 0)),
                   pl.BlockSpec((1, tm, E), lambda b, i: (b, i, 0))],
        out_shape=[jax.ShapeDtypeStruct((B, S, D), F32),
                   jax.ShapeDtypeStruct((B, S, D), BF16),
                   jax.ShapeDtypeStruct((B, S, E), F32)],
        compiler_params=_cparams(("arbitrary", "arbitrary")),
        name="outproj",
    )(ym, ya, x, mod, wom, woa, g2, wrh, wrl)


def _topk_body(a_ref, rank_ref, starts_ref, *, S, cap, tb):
    bits = lax.bitcast_convert_type(a_ref[0], I32)
    E = bits.shape[0]

    def search(i, thr):
        cand = thr | jnp.left_shift(jnp.int32(1), 30 - i)
        cnt = jnp.sum((bits >= cand).astype(I32), axis=1, keepdims=True)
        return jnp.where(cnt >= cap, cand, thr)
    thr = lax.fori_loop(0, 31, search, jnp.zeros((E, 1), I32))

    gt = bits > thr
    eq = bits == thr
    need = cap - jnp.sum(gt.astype(I32), axis=1, keepdims=True)

    nt = S // tb
    upper = (lax.broadcasted_iota(I32, (tb, tb), 0) < lax.broadcasted_iota(I32, (tb, tb), 1)).astype(BF16)

    def excl_cumsum(mask, with_starts):
        off = jnp.zeros((E, 1), F32)
        outs, starts = [], []
        for t in range(nt):
            blk = mask[:, t * tb:(t + 1) * tb].astype(F32).astype(BF16)
            outs.append(jnp.dot(blk, upper, preferred_element_type=F32) + off)
            starts.append(off)
            off = off + jnp.sum(blk.astype(F32), axis=1, keepdims=True)
        res = jnp.concatenate(outs, axis=1).astype(I32)
        return (res, jnp.concatenate(starts, axis=1).astype(I32)) if with_starts else res

    sel = gt | (eq & (excl_cumsum(eq, False) < need))
    rank, starts = excl_cumsum(sel, True)
    rank_ref[0] = jnp.where(sel, rank, -1)
    starts_ref[0] = starts


def _topk(aff_t, cap):
    B, E, S = aff_t.shape
    tb = min(TOK_TILE, S)
    nt = S // tb
    body = functools.partial(_topk_body, S=S, cap=cap, tb=tb)
    return pl.pallas_call(
        body,
        grid=(B,),
        in_specs=[pl.BlockSpec((1, E, S), lambda b: (b, 0, 0))],
        out_specs=[pl.BlockSpec((1, E, S), lambda b: (b, 0, 0)),
                   pl.BlockSpec((1, E, nt), lambda b: (b, 0, 0))],
        out_shape=[jax.ShapeDtypeStruct((B, E, S), I32), jax.ShapeDtypeStruct((B, E, nt), I32)],
        compiler_params=_cparams(("arbitrary",)),
        name="topk",
    )(aff_t)


def _gather_body(starts_ref, h2_ref, rank_ref, xe_ref, acc_s, *, S, E, cap, tb):
    b = pl.program_id(0)
    e = pl.program_id(1)
    nt = S // tb
    win = tb + BF16_ROWS
    D = acc_s.shape[1]
    sub = lax.broadcasted_iota(I32, (win, tb), 0)
    head = lax.broadcasted_iota(I32, (BF16_ROWS, D), 0)

    def tile(t, _):
        r0 = starts_ref[(b * E + e) * nt + t]
        r0a = pl.multiple_of((r0 // BF16_ROWS) * BF16_ROWS, BF16_ROWS)
        off = pl.multiple_of(t * tb, tb)
        rel = rank_ref[0, 0, :, pl.ds(off, tb)] - r0a
        onehot = (sub == rel).astype(F32).astype(BF16)
        rows = jnp.dot(onehot, h2_ref[0, pl.ds(off, tb), :], preferred_element_type=F32).astype(BF16)
        keep = acc_s[pl.ds(r0a, BF16_ROWS), :]
        first = jnp.where(head < r0 - r0a, keep, rows[0:BF16_ROWS, :])
        acc_s[pl.ds(r0a, BF16_ROWS), :] = first
        acc_s[pl.ds(r0a + BF16_ROWS, tb), :] = rows[BF16_ROWS:, :]
        return 0

    acc_s[0:BF16_ROWS, :] = jnp.zeros((BF16_ROWS, D), BF16)
    lax.fori_loop(0, nt, tile, 0)
    xe_ref[0] = acc_s[0:cap, :]


def _gather(starts, h2, rank_t, cap):
    B, S, D = h2.shape
    E = rank_t.shape[1]
    tb = min(TOK_TILE, S)
    body = functools.partial(_gather_body, S=S, E=E, cap=cap, tb=tb)
    grid_spec = pltpu.PrefetchScalarGridSpec(
        num_scalar_prefetch=1,
        grid=(B, E),
        in_specs=[pl.BlockSpec((1, S, D), lambda b, e, st: (b, 0, 0)),
                  pl.BlockSpec((1, 1, 1, S), lambda b, e, st: (b, e, 0, 0))],
        out_specs=pl.BlockSpec((1, cap, D), lambda b, e, st: (e, b, 0)),
        scratch_shapes=[pltpu.VMEM((cap + tb + BF16_ROWS, D), BF16)],
    )
    return pl.pallas_call(
        body,
        grid_spec=grid_spec,
        out_shape=jax.ShapeDtypeStruct((E, B * cap, D), BF16),
        compiler_params=_cparams(("arbitrary", "arbitrary")),
        name="gather",
    )(starts.reshape(-1), h2, rank_t.reshape(B, E, 1, S))


def _ffn_body(x_ref, wg_ref, wu_ref, wd_ref, y_ref, acc_s, wg_s, wu_s, wd_s, *, tt, sub):
    fc = pl.program_id(2)
    wg_s[...] = wg_ref[0].astype(BF16)
    wu_s[...] = wu_ref[0].astype(BF16)
    wd_s[...] = wd_ref[0].astype(BF16)

    def rows(i, _):
        off = pl.multiple_of(i * sub, sub)
        x = x_ref[0, pl.ds(off, sub), :]
        g = jnp.dot(x, wg_s[...], preferred_element_type=F32)
        u = jnp.dot(x, wu_s[...], preferred_element_type=F32)
        a = (g * _sigmoid(g) * u).astype(BF16)
        part = jnp.dot(a, wd_s[...], preferred_element_type=F32)

        @pl.when(fc == 0)
        def _():
            acc_s[pl.ds(off, sub), :] = part

        @pl.when(fc > 0)
        def _():
            acc_s[pl.ds(off, sub), :] += part
        return 0
    lax.fori_loop(0, tt // sub, rows, 0)

    @pl.when(fc == pl.num_programs(2) - 1)
    def _():
        y_ref[0] = acc_s[...].astype(BF16)


def _ffn(xe, w_gate, w_up, w_down):
    E, T, D = xe.shape
    F = w_gate.shape[2]
    tt = min(2048, T)
    tf = min(512, F)
    sub = min(256, tt)
    body = functools.partial(_ffn_body, tt=tt, sub=sub)
    return pl.pallas_call(
        body,
        grid=(E, T // tt, F // tf),
        in_specs=[pl.BlockSpec((1, tt, D), lambda e, t, f: (e, t, 0)),
                  pl.BlockSpec((1, D, tf), lambda e, t, f: (e, 0, f)),
                  pl.BlockSpec((1, D, tf), lambda e, t, f: (e, 0, f)),
                  pl.BlockSpec((1, tf, D), lambda e, t, f: (e, f, 0))],
        out_specs=pl.BlockSpec((1, tt, D), lambda e, t, f: (e, t, 0)),
        out_shape=jax.ShapeDtypeStruct((E, T, D), BF16),
        scratch_shapes=[pltpu.VMEM((tt, D), F32), pltpu.VMEM((D, tf), BF16), pltpu.VMEM((D, tf), BF16),
                        pltpu.VMEM((tf, D), BF16)],
        compiler_params=_cparams(("arbitrary", "arbitrary", "arbitrary")),
        name="ffn",
    )(xe, w_gate, w_up, w_down)


def _combine_body(starts_ref, y_ref, rank_ref, aff_ref, x1_ref, mod_ref, o_ref, *, S, E, cap, tb):
    b = pl.program_id(0)
    t = pl.program_id(1)
    nt = S // tb
    kw = min(tb, cap)
    lane = lax.broadcasted_iota(I32, (tb, kw), 1)
    rank = rank_ref[0]
    aff = aff_ref[0]
    acc = jnp.zeros(o_ref.shape[1:], F32)
    for e in range(E):
        row = (b * E + e) * nt
        r0 = starts_ref[row + t]
        r_end = jnp.where(t == nt - 1, cap, starts_ref[row + jnp.minimum(t + 1, nt - 1)])
        r0a = jnp.minimum((r0 // BF16_ROWS) * BF16_ROWS, cap - kw)
        r0a = pl.multiple_of(r0a, BF16_ROWS)
        rk = rank[:, e:e + 1]
        onehot = (rk - r0a == lane).astype(F32).astype(BF16)
        part = jnp.dot(onehot, y_ref[e, pl.ds(r0a, kw), :], preferred_element_type=F32)
        if cap > kw:
            r1a = pl.multiple_of(jnp.minimum(r0a + kw, cap - kw), BF16_ROWS)

            def second(rk=rk, r0a=r0a, r1a=r1a, e=e):
                oh = ((rk - r1a == lane) & (rk >= r0a + kw)).astype(F32).astype(BF16)
                return jnp.dot(oh, y_ref[e, pl.ds(r1a, kw), :], preferred_element_type=F32)

            part = part + lax.cond(r_end > r0a + kw, second, lambda: jnp.zeros((tb, y_ref.shape[2]), F32))
        acc = acc + aff[:, e:e + 1] * part
    o_ref[0] = x1_ref[0] + mod_ref[0, 5:6, :] * acc


def _combine(starts, y, rank, aff, x1, mod, cap):
    B, S, D = x1.shape
    E = aff.shape[2]
    tb = min(TOK_TILE, S)
    body = functools.partial(_combine_body, S=S, E=E, cap=cap, tb=tb)
    grid_spec = pltpu.PrefetchScalarGridSpec(
        num_scalar_prefetch=1,
        grid=(B, S // tb),
        in_specs=[pl.BlockSpec((E, cap, D), lambda b, t, st: (0, b, 0)),
                  pl.BlockSpec((1, tb, E), lambda b, t, st: (b, t, 0)),
                  pl.BlockSpec((1, tb, E), lambda b, t, st: (b, t, 0)),
                  pl.BlockSpec((1, tb, D), lambda b, t, st: (b, t, 0)),
                  pl.BlockSpec((1, 6, D), lambda b, t, st: (b, 0, 0))],
        out_specs=pl.BlockSpec((1, tb, D), lambda b, t, st: (b, t, 0)),
    )
    return pl.pallas_call(
        body,
        grid_spec=grid_spec,
        out_shape=jax.ShapeDtypeStruct((B, S, D), F32),
        compiler_params=_cparams(("arbitrary", "arbitrary")),
        name="combine",
    )(starts.reshape(-1), y, rank, aff, x1, mod)


def _layer(layer_idx, x, c, positions, norm1_g, norm2_g, w_ada, b_ada, w_in, conv_w, gate_b, mnorm_g, qk_g,
           lam_p, subln_g, w_out, w_router, w_gate_e, w_up_e, w_down_e):
    B, S, D = x.shape
    E = w_router.shape[1]
    d_m = mnorm_g.shape[0]
    n_g = 4 * M_HEADS
    d_a = w_out.shape[0] - d_m
    cap = CAP_FACTOR * S // E
    L = min(MLSTM_CHUNK, S)

    mod = _ada(c, w_ada, b_ada).reshape(B, 6, D)

    wm = w_in[:, :4 * d_m].astype(BF16)
    wg = w_in[:, 4 * d_m:4 * d_m + n_g].astype(BF16)
    wa = w_in[:, 4 * d_m + n_g:].astype(BF16)
    pm, pa, gates = _inproj(x, mod, norm1_g.reshape(1, D), wm, wa, wg, gate_b.reshape(1, n_g))

    gt = gates.reshape(B, S // L, L, n_g).transpose(0, 3, 1, 2)
    ym = _mlstm(pm, conv_w, gt, mnorm_g.reshape(1, d_m))

    qn, kn = _qkprep(pa, positions, qk_g)
    ya = _attn(qn, kn, pa, lam_p, subln_g, layer_idx)

    wo = w_out.astype(BF16)
    wr_hi = w_router.astype(BF16)
    wr_lo = (w_router - wr_hi.astype(F32)).astype(BF16)
    x1, h2, aff = _outproj(ym, ya, x, mod, wo[:d_m], wo[d_m:], norm2_g.reshape(1, D), wr_hi, wr_lo)

    rank_t, starts = _topk(aff.transpose(0, 2, 1), cap)
    xe = _gather(starts, h2, rank_t, cap)
    y = _ffn(xe, w_gate_e, w_up_e, w_down_e)
    return _combine(starts, y, rank_t.transpose(0, 2, 1), aff, x1, mod, cap)


def kernel(x, c, positions, norm1_g, norm2_g, w_ada, b_ada, w_in, mlstm_conv_w, mlstm_gate_b, mlstm_norm_g,
           diff_qk_g, diff_lambda, diff_subln_g, w_out, w_router, w_gate_e, w_up_e, w_down_e):
    for l in range(norm1_g.shape[0]):
        x = _layer(l, x, c, positions, norm1_g[l], norm2_g[l], w_ada[l], b_ada[l], w_in[l], mlstm_conv_w[l],
                   mlstm_gate_b[l], mlstm_norm_g[l], diff_qk_g[l], diff_lambda[l], diff_subln_g[l], w_out[l],
                   w_router[l], w_gate_e[l], w_up_e[l], w_down_e[l])
    return x
```

```python
import functools
import math

import numpy as np
import jax
import jax.numpy as jnp
from jax import lax
from jax.experimental import pallas as pl
from jax.experimental.pallas import tpu as pltpu

F32 = jnp.float32
BF16 = jnp.bfloat16
I32 = jnp.int32

M_HEADS = 4
A_HEADS = 4
CONV_K = 5
ROPE_THETA = 500000.0
EPS = 1e-6
CAP_FACTOR = 2

MLSTM_CHUNK = 256
SCAN_UNROLL = 2
ROW_TILE = 512
ATT_TQ = 256
ATT_TK = 512
TOK_TILE = 256
BF16_ROWS = 16
AFF_LANES = 128
GATHER_UNROLL = 4
FFN_ROWS = 2048
FFN_HIDDEN = 512
FFN_SUB = 512
VMEM_LIMIT = 56 * 1024 * 1024


def _cparams(sem):
    return pltpu.CompilerParams(dimension_semantics=sem, vmem_limit_bytes=VMEM_LIMIT)


def _sigmoid(v):
    return 1.0 / (1.0 + jnp.exp(-v))


def _rms(v):
    return v * lax.rsqrt(jnp.mean(v * v, axis=-1, keepdims=True) + EPS)


def _ada_body(c_ref, w_ref, b_ref, o_ref):
    c = c_ref[...]
    s = (c * _sigmoid(c)).astype(BF16)
    o_ref[...] = jnp.dot(s, w_ref[...].astype(BF16), preferred_element_type=F32) + b_ref[...]


def _ada(c, w_ada, b_ada):
    B, D = c.shape
    N = w_ada.shape[1]
    tn = D
    return pl.pallas_call(
        _ada_body,
        grid=(N // tn,),
        in_specs=[pl.BlockSpec((B, D), lambda j: (0, 0)),
                  pl.BlockSpec((D, tn), lambda j: (0, j)),
                  pl.BlockSpec((1, tn), lambda j: (0, j))],
        out_specs=pl.BlockSpec((B, tn), lambda j: (0, j)),
        out_shape=jax.ShapeDtypeStruct((B, N), F32),
        compiler_params=_cparams(("arbitrary",)),
        name="ada",
    )(c, w_ada, b_ada.reshape(1, N))


def _inproj_body(x_ref, mod_ref, g_ref, wm_ref, wa_ref, wg_ref, bg_ref, pm_ref, pa_ref, gt_ref):
    x = x_ref[0]
    h = _rms(x) * g_ref[...]
    h = (h * (1.0 + mod_ref[0, 1:2, :]) + mod_ref[0, 0:1, :]).astype(BF16)
    pm_ref[0] = jnp.dot(h, wm_ref[...], preferred_element_type=F32).astype(BF16)
    pa_ref[0] = jnp.dot(h, wa_ref[...], preferred_element_type=F32).astype(BF16)
    gt_ref[0] = jnp.dot(h, wg_ref[...], preferred_element_type=F32) + bg_ref[...]


def _inproj(x, mod, g1, wm, wa, wg, bg):
    B, S, D = x.shape
    tm = min(ROW_TILE, S)
    nm, na, ng = wm.shape[1], wa.shape[1], wg.shape[1]
    return pl.pallas_call(
        _inproj_body,
        grid=(B, S // tm),
        in_specs=[pl.BlockSpec((1, tm, D), lambda b, i: (b, i, 0)),
                  pl.BlockSpec((1, 6, D), lambda b, i: (b, 0, 0)),
                  pl.BlockSpec((1, D), lambda b, i: (0, 0)),
                  pl.BlockSpec((D, nm), lambda b, i: (0, 0)),
                  pl.BlockSpec((D, na), lambda b, i: (0, 0)),
                  pl.BlockSpec((D, ng), lambda b, i: (0, 0)),
                  pl.BlockSpec((1, ng), lambda b, i: (0, 0))],
        out_specs=[pl.BlockSpec((1, tm, nm), lambda b, i: (b, i, 0)),
                   pl.BlockSpec((1, tm, na), lambda b, i: (b, i, 0)),
                   pl.BlockSpec((1, tm, ng), lambda b, i: (b, i, 0))],
        out_shape=[jax.ShapeDtypeStruct((B, S, nm), BF16),
                   jax.ShapeDtypeStruct((B, S, na), BF16),
                   jax.ShapeDtypeStruct((B, S, ng), F32)],
        compiler_params=_cparams(("arbitrary", "arbitrary")),
        name="inproj",
    )(x, mod, g1, wm, wa, wg, bg)


def _log_sigmoid(v):
    return jnp.minimum(v, 0.0) - jnp.log(1.0 + jnp.exp(-jnp.abs(v)))


def _mlstm_body(q_ref, k_ref, v_ref, o_ref, cwq_ref, cwk_ref, gt_ref, ng_ref, y_ref,
                xp_s, q_s, kt_s, va_s, hf_s, a_s, *, S, L, dh):
    hd = pl.program_id(1)
    nc = S // L
    win = L + 16

    def conv(src_ref, cw_ref, post):
        xp_s[0:8, :] = jnp.zeros((8, dh), F32)
        xp_s[S + 8:S + 16, :] = jnp.zeros((8, dh), F32)

        def fill(c, _):
            base = pl.multiple_of(c * L, L)
            xp_s[pl.ds(pl.multiple_of(base + 8, 8), L), :] = src_ref[0, pl.ds(base, L), :].astype(F32)
            return 0
        lax.fori_loop(0, nc, fill, 0)

        def body(c, _):
            base = pl.multiple_of(c * L, L)
            w = xp_s[pl.ds(base, win), :]
            acc = None
            for j in range(CONV_K):
                sh = (CONV_K // 2 - j) % win
                r = w if sh == 0 else pltpu.roll(w, sh, 0)
                term = r[8:8 + L, :] * cw_ref[j:j + 1, :]
                acc = term if acc is None else acc + term
            post(base, acc * _sigmoid(acc))
            return 0
        lax.fori_loop(0, nc, body, 0)

    def post_q(base, a):
        q_s[pl.ds(base, L), :] = a.astype(BF16)

    def post_k(base, a):
        kt_s[:, pl.ds(base, L)] = (a * (dh ** -0.5)).T.astype(BF16)

    conv(q_ref, cwq_ref, post_q)
    conv(k_ref, cwk_ref, post_k)

    ones_col = (lax.broadcasted_iota(I32, (L, dh), 1) == 0).astype(BF16)

    def fill_v(c, _):
        base = pl.multiple_of(c * L, L)
        va_s[pl.ds(base, L), 0:dh] = v_ref[0, pl.ds(base, L), :]
        va_s[pl.ds(base, L), dh:2 * dh] = ones_col
        return 0
    lax.fori_loop(0, nc, fill_v, 0)

    ri = lax.broadcasted_iota(I32, (L, L), 0)
    ci = lax.broadcasted_iota(I32, (L, L), 1)

    def scan(reverse, gi, gf, emit):
        logf_all = _log_sigmoid(gt_ref[0, gf])
        tri = ((ri >= ci) if reverse else (ri <= ci)).astype(F32)
        a_s[...] = jnp.dot(logf_all, tri, preferred_element_type=F32, precision=lax.Precision.HIGHEST)
        valid = (ci >= ri) if reverse else (ci <= ri)

        def body(step, carry):
            caug, m = carry
            c = (nc - 1 - step) if reverse else step
            base = pl.multiple_of(c * L, L)
            i_row = gt_ref[0, gi, pl.ds(c, 1), :]
            logf = _log_sigmoid(gt_ref[0, gf, pl.ds(c, 1), :])
            a_row = a_s[pl.ds(c, 1), :]
            a_col = jnp.sum(jnp.where(valid, logf, 0.0), axis=1, keepdims=True)
            g = jnp.sum(logf, axis=1, keepdims=True)
            dmat = jnp.where(valid, a_col - a_row + i_row, -jnp.inf)
            dmax = jnp.max(dmat, axis=1, keepdims=True)
            q_c = q_s[pl.ds(base, L), :]
            kt_c = kt_s[:, pl.ds(base, L)]
            va_c = va_s[pl.ds(base, L), :]
            qk = jnp.dot(q_c, kt_c, preferred_element_type=F32)
            p = (qk * jnp.exp(dmat - dmax)).astype(BF16)
            intra = jnp.dot(p, va_c, preferred_element_type=F32)
            inter = a_col + m
            mj = jnp.maximum(inter, dmax)
            nd = (jnp.exp(dmax - mj) * intra
                  + jnp.exp(inter - mj) * jnp.dot(q_c, caug.astype(BF16), preferred_element_type=F32))
            den = jnp.maximum(jnp.abs(nd[:, dh:dh + 1]), jnp.exp(-mj))
            emit(base, nd[:, 0:dh] * (1.0 / den))
            wlog = g - a_row + i_row
            mloc = jnp.max(wlog, axis=1, keepdims=True)
            kw = (kt_c.astype(F32) * jnp.exp(wlog - mloc)).astype(BF16)
            cloc = jnp.dot(kw, va_c, preferred_element_type=F32)
            m_new = jnp.maximum(g + m, mloc)
            caug = jnp.exp(g + m - m_new) * caug + jnp.exp(mloc - m_new) * cloc
            return caug, m_new

        lax.fori_loop(0, nc, body, (jnp.zeros((dh, 2 * dh), F32), jnp.zeros((1, 1), F32)),
                      unroll=min(SCAN_UNROLL, nc))

    def emit_f(base, h):
        hf_s[pl.ds(base, L), :] = h

    def emit_b(base, h):
        hn = _rms(hf_s[pl.ds(base, L), :] + h) * ng_ref[...]
        og = o_ref[0, pl.ds(base, L), :].astype(F32)
        y_ref[0, pl.ds(base, L), :] = (_sigmoid(og) * hn).astype(BF16)

    scan(False, hd, M_HEADS + hd, emit_f)
    scan(True, 2 * M_HEADS + hd, 3 * M_HEADS + hd, emit_b)


def _mlstm(pm, conv_w, gt, norm_g):
    B, S, _ = pm.shape
    dh = norm_g.shape[1] // M_HEADS
    L = min(MLSTM_CHUNK, S)
    nc = S // L
    H = M_HEADS
    body = functools.partial(_mlstm_body, S=S, L=L, dh=dh)
    return pl.pallas_call(
        body,
        grid=(B, H),
        in_specs=[pl.BlockSpec((1, S, dh), lambda b, h: (b, 0, h)),
                  pl.BlockSpec((1, S, dh), lambda b, h: (b, 0, H + h)),
                  pl.BlockSpec((1, S, dh), lambda b, h: (b, 0, 2 * H + h)),
                  pl.BlockSpec((1, S, dh), lambda b, h: (b, 0, 3 * H + h)),
                  pl.BlockSpec((CONV_K, dh), lambda b, h: (0, h)),
                  pl.BlockSpec((CONV_K, dh), lambda b, h: (0, H + h)),
                  pl.BlockSpec((1, 4 * H, nc, L), lambda b, h: (b, 0, 0, 0)),
                  pl.BlockSpec((1, dh), lambda b, h: (0, h))],
        out_specs=pl.BlockSpec((1, S, dh), lambda b, h: (b, 0, h)),
        out_shape=jax.ShapeDtypeStruct((B, S, H * dh), BF16),
        scratch_shapes=[pltpu.VMEM((S + 16, dh), F32),
                        pltpu.VMEM((S, dh), BF16),
                        pltpu.VMEM((dh, S), BF16),
                        pltpu.VMEM((S, 2 * dh), BF16),
                        pltpu.VMEM((S, dh), F32),
                        pltpu.VMEM((nc, L), F32)],
        compiler_params=_cparams(("arbitrary", "arbitrary")),
        name="mlstm",
    )(pm, pm, pm, pm, conv_w, conv_w, gt, norm_g)


def _qkprep_body(q_ref, k_ref, pos_ref, invf_ref, s1_ref, s2_ref, gm_ref, gq_ref, gk_ref,
                 qo_ref, ko_ref, *, groups, q_scale):
    ang = pos_ref[0].astype(F32) * invf_ref[...]
    cs = jnp.cos(ang)
    sn = jnp.sin(ang)
    s1 = s1_ref[...]
    s2 = s2_ref[...]
    gm = gm_ref[...]
    lanes = gm.shape[0]

    def prep(src_ref, g_ref, dst_ref, scale):
        for j in range(groups):
            x = src_ref[0, :, j * lanes:(j + 1) * lanes].astype(F32)
            ms = jnp.dot((x * x).astype(BF16), gm, preferred_element_type=F32)
            y = x * lax.rsqrt(ms + EPS) * g_ref[...]
            half = pltpu.roll(y, lanes - 8, 1) * s1 + pltpu.roll(y, 8, 1) * s2
            r = y * cs + half * sn
            dst_ref[0, :, j * lanes:(j + 1) * lanes] = (r * scale).astype(BF16)

    prep(q_ref, gq_ref, qo_ref, q_scale)
    prep(k_ref, gk_ref, ko_ref, 1.0)


def _qkprep(pa, positions, qk_g):
    B, S, _ = pa.shape
    hd = qk_g.shape[1]
    rot = hd // 4
    half = rot // 2
    lanes = 128
    W = A_HEADS * 2 * hd
    tm = min(ROW_TILE, S)
    d = np.arange(lanes) % hd
    inv_freq = (ROPE_THETA ** (-np.arange(0, rot, 2, dtype=np.float32) / rot)).astype(np.float32)
    invf = np.where(d < rot, inv_freq[d % half], 0.0).astype(np.float32)[None, :]
    s1 = np.where(d < half, -1.0, 0.0).astype(np.float32)[None, :]
    s2 = np.where((d >= half) & (d < rot), 1.0, 0.0).astype(np.float32)[None, :]
    gm = (np.arange(lanes)[:, None] // hd == np.arange(lanes)[None, :] // hd).astype(np.float32) / hd
    gq = jnp.tile(qk_g[0], lanes // hd)[None, :]
    gk = jnp.tile(qk_g[1], lanes // hd)[None, :]
    q_scale = (hd ** -0.5) * math.log2(math.e)
    body = functools.partial(_qkprep_body, groups=W // lanes, q_scale=q_scale)
    small = lambda shape: pl.BlockSpec(shape, lambda b, i: (0, 0))
    return pl.pallas_call(
        body,
        grid=(B, S // tm),
        in_specs=[pl.BlockSpec((1, tm, W), lambda b, i: (b, i, 0)),
                  pl.BlockSpec((1, tm, W), lambda b, i: (b, i, 1)),
                  pl.BlockSpec((1, tm, 1), lambda b, i: (b, i, 0)),
                  small((1, lanes)), small((1, lanes)), small((1, lanes)), small((lanes, lanes)),
                  small((1, lanes)), small((1, lanes))],
        out_specs=[pl.BlockSpec((1, tm, W), lambda b, i: (b, i, 0)),
                   pl.BlockSpec((1, tm, W), lambda b, i: (b, i, 0))],
        out_shape=[jax.ShapeDtypeStruct((B, S, W), BF16), jax.ShapeDtypeStruct((B, S, W), BF16)],
        compiler_params=_cparams(("arbitrary", "arbitrary")),
        name="qkprep",
    )(pa, pa, positions.reshape(B, S, 1), jnp.asarray(invf), jnp.asarray(s1), jnp.asarray(s2),
      jnp.asarray(gm, dtype=BF16), gq, gk)


def _attn_body(q_ref, k_ref, v_ref, lam_ref, sg_ref, y_ref, va_s, s0_s, s1_s, *, S, tq, tk, dv, lam_init):
    nk = S // tk
    lanes = 128

    @pl.when(pl.program_id(2) == 0)
    def _():
        ones_col = (lax.broadcasted_iota(I32, (tk, dv), 1) == 0).astype(BF16)
        for kb in range(nk):
            va_s[kb * tk:(kb + 1) * tk, 0:dv] = v_ref[0, kb * tk:(kb + 1) * tk, :]
            va_s[kb * tk:(kb + 1) * tk, dv:2 * dv] = ones_col

    lp = lam_ref[...]
    lam = (jnp.exp(jnp.sum(lp[0:1, :] * lp[1:2, :], axis=1, keepdims=True))
           - jnp.exp(jnp.sum(lp[2:3, :] * lp[3:4, :], axis=1, keepdims=True)) + lam_init)

    q = q_ref[0]
    lane = lax.broadcasted_iota(I32, q.shape, 1)
    hd = q.shape[1] // 2

    def softmax_pv(qm, s_s):
        mpart = None
        for kb in range(nk):
            s = lax.dot_general(qm, k_ref[0, kb * tk:(kb + 1) * tk, :], (((1,), (1,)), ((), ())),
                                preferred_element_type=F32)
            s_s[:, kb * tk:(kb + 1) * tk] = s
            for j in range(tk // lanes):
                blk = s[:, j * lanes:(j + 1) * lanes]
                mpart = blk if mpart is None else jnp.maximum(mpart, blk)
        mb = jnp.broadcast_to(jnp.max(mpart, axis=1, keepdims=True), (tq, lanes))
        acc = None
        for kb in range(nk):
            p = jnp.concatenate(
                [jnp.exp2(s_s[:, kb * tk + j * lanes:kb * tk + (j + 1) * lanes] - mb).astype(BF16)
                 for j in range(tk // lanes)], axis=1)
            part = jnp.dot(p, va_s[kb * tk:(kb + 1) * tk, :], preferred_element_type=F32)
            acc = part if acc is None else acc + part
        return acc[:, 0:dv] * (1.0 / acc[:, dv:dv + 1])

    o = softmax_pv(jnp.where(lane < hd, q, jnp.zeros_like(q)), s0_s) \
        - lam * softmax_pv(jnp.where(lane >= hd, q, jnp.zeros_like(q)), s1_s)
    y_ref[0] = (_rms(o) * sg_ref[...] * (1.0 - lam_init)).astype(BF16)


def _attn(qn, kn, pa, lam_p, subln_g, layer_idx):
    B, S, W = qn.shape
    dv = W // A_HEADS
    tq = min(ATT_TQ, S)
    tk = min(ATT_TK, S)
    lam_init = 0.8 - 0.6 * math.exp(-0.3 * layer_idx)
    vcol0 = 2 * W // dv
    body = functools.partial(_attn_body, S=S, tq=tq, tk=tk, dv=dv, lam_init=lam_init)
    return pl.pallas_call(
        body,
        grid=(B, A_HEADS, S // tq),
        in_specs=[pl.BlockSpec((1, tq, dv), lambda b, h, i: (b, i, h)),
                  pl.BlockSpec((1, S, dv), lambda b, h, i: (b, 0, h)),
                  pl.BlockSpec((1, S, dv), lambda b, h, i: (b, 0, vcol0 + h)),
                  pl.BlockSpec(lam_p.shape, lambda b, h, i: (0, 0)),
                  pl.BlockSpec((1, dv), lambda b, h, i: (0, 0))],
        out_specs=pl.BlockSpec((1, tq, dv), lambda b, h, i: (b, i, h)),
        out_shape=jax.ShapeDtypeStruct((B, S, W), BF16),
        scratch_shapes=[pltpu.VMEM((S, 2 * dv), BF16), pltpu.VMEM((tq, S), F32), pltpu.VMEM((tq, S), F32)],
        compiler_params=_cparams(("arbitrary", "arbitrary", "arbitrary")),
        name="attn",
    )(qn, kn, pa, lam_p, subln_g.reshape(1, dv))


def _outproj_body(ym_ref, ya_ref, x_ref, mod_ref, wom_ref, woa_ref, g2_ref, wrh_ref, wrl_ref,
                  x1_ref, h2_ref, aff_ref):
    mix = (jnp.dot(ym_ref[0], wom_ref[...], preferred_element_type=F32)
           + jnp.dot(ya_ref[0], woa_ref[...], preferred_element_type=F32))
    x1 = x_ref[0] + mod_ref[0, 2:3, :] * mix
    x1_ref[0] = x1
    h2 = _rms(x1) * g2_ref[...] * (1.0 + mod_ref[0, 4:5, :]) + mod_ref[0, 3:4, :]
    hh = h2.astype(BF16)
    h2_ref[0] = hh
    hl = (h2 - hh.astype(F32)).astype(BF16)
    logits = (jnp.dot(hh, wrh_ref[...], preferred_element_type=F32)
              + jnp.dot(hl, wrh_ref[...], preferred_element_type=F32)
              + jnp.dot(hh, wrl_ref[...], preferred_element_type=F32))
    ex = jnp.exp(logits - jnp.max(logits, axis=1, keepdims=True))
    aff_ref[0] = ex * (1.0 / jnp.sum(ex, axis=1, keepdims=True))


def _outproj(ym, ya, x, mod, wom, woa, g2, wrh, wrl):
    B, S, D = x.shape
    tm = min(ROW_TILE, S)
    wm, wa, E = ym.shape[2], ya.shape[2], wrh.shape[1]
    full = lambda shape: pl.BlockSpec(shape, lambda b, i: (0, 0))
    return pl.pallas_call(
        _outproj_body,
        grid=(B, S // tm),
        in_specs=[pl.BlockSpec((1, tm, wm), lambda b, i: (b, i, 0)),
                  pl.BlockSpec((1, tm, wa), lambda b, i: (b, i, 0)),
                  pl.BlockSpec((1, tm, D), lambda b, i: (b, i, 0)),
                  pl.BlockSpec((1, 6, D), lambda b, i: (b, 0, 0)),
                  full((wm, D)), full((wa, D)), full((1, D)), full((D, E)), full((D, E))],
        out_specs=[pl.BlockSpec((1, tm, D), lambda b, i: (b, i, 0)),
                   pl.BlockSpec((1, tm, D), lambda b, i: (b, i, 0)),
                   pl.BlockSpec((1, tm, E), lambda b, i: (b, i, 0))],
        out_shape=[jax.ShapeDtypeStruct((B, S, D), F32),
                   jax.ShapeDtypeStruct((B, S, D), BF16),
                   jax.ShapeDtypeStruct((B, S, E), F32)],
        compiler_params=_cparams(("arbitrary", "arbitrary")),
        name="outproj",
    )(ym, ya, x, mod, wom, woa, g2, wrh, wrl)


def _topk_body(a_ref, rank_ref, starts_ref, *, S, cap, tb):
    bits = lax.bitcast_convert_type(a_ref[0], I32)
    E = bits.shape[0]

    def search(i, thr):
        cand = thr | jnp.left_shift(jnp.int32(1), 30 - i)
        cnt = jnp.sum((bits >= cand).astype(I32), axis=1, keepdims=True)
        return jnp.where(cnt >= cap, cand, thr)
    thr = lax.fori_loop(0, 31, search, jnp.zeros((E, 1), I32))

    gt = bits > thr
    eq = bits == thr
    need = cap - jnp.sum(gt.astype(I32), axis=1, keepdims=True)

    nt = S // tb
    upper = (lax.broadcasted_iota(I32, (tb, tb), 0) < lax.broadcasted_iota(I32, (tb, tb), 1)).astype(BF16)

    def excl_cumsum(mask, with_starts):
        off = jnp.zeros((E, 1), F32)
        outs, starts = [], []
        for t in range(nt):
            blk = mask[:, t * tb:(t + 1) * tb].astype(F32).astype(BF16)
            outs.append(jnp.dot(blk, upper, preferred_element_type=F32) + off)
            starts.append(off)
            off = off + jnp.sum(blk.astype(F32), axis=1, keepdims=True)
        res = jnp.concatenate(outs, axis=1).astype(I32)
        return (res, jnp.concatenate(starts, axis=1).astype(I32)) if with_starts else res

    sel = gt | (eq & (excl_cumsum(eq, False) < need))
    rank, starts = excl_cumsum(sel, True)
    rank_ref[0] = jnp.where(sel, rank, -1)
    starts_ref[0] = starts


def _topk(aff_t, cap):
    B, E, S = aff_t.shape
    tb = min(TOK_TILE, S)
    nt = S // tb
    body = functools.partial(_topk_body, S=S, cap=cap, tb=tb)
    return pl.pallas_call(
        body,
        grid=(B,),
        in_specs=[pl.BlockSpec((1, E, S), lambda b: (b, 0, 0))],
        out_specs=[pl.BlockSpec((1, E, S), lambda b: (b, 0, 0)),
                   pl.BlockSpec((1, E, nt), lambda b: (b, 0, 0))],
        out_shape=[jax.ShapeDtypeStruct((B, E, S), I32), jax.ShapeDtypeStruct((B, E, nt), I32)],
        compiler_params=_cparams(("arbitrary",)),
        name="topk",
    )(aff_t)


def _gather_body(starts_ref, h2_ref, aff_ref, rank_ref, xe_ref, acc_s, *, S, E, cap, tb):
    b = pl.program_id(0)
    e = pl.program_id(1)
    nt = S // tb
    win = tb + BF16_ROWS
    DX = acc_s.shape[1]
    sub = lax.broadcasted_iota(I32, (win, tb), 0)
    head = lax.broadcasted_iota(I32, (BF16_ROWS, DX), 0)
    elane = lax.broadcasted_iota(I32, (tb, E), 1)
    xlane = lax.broadcasted_iota(I32, (tb, AFF_LANES), 1)

    def tile(t, _):
        r0 = starts_ref[(b * E + e) * nt + t]
        r0a = pl.multiple_of((r0 // BF16_ROWS) * BF16_ROWS, BF16_ROWS)
        off = pl.multiple_of(t * tb, tb)
        rel = rank_ref[0, 0, :, pl.ds(off, tb)] - r0a
        onehot = (sub == rel).astype(F32).astype(BF16)
        a = jnp.sum(jnp.where(elane == e, aff_ref[0, pl.ds(off, tb), :], 0.0), axis=1, keepdims=True)
        a1 = a.astype(BF16).astype(F32)
        a2 = (a - a1).astype(BF16).astype(F32)
        a3 = a - a1 - a2
        ext = jnp.where(xlane == 0, a1, jnp.where(xlane == 1, a2, jnp.where(xlane == 2, a3, 0.0))).astype(BF16)
        src = jnp.concatenate([h2_ref[0, pl.ds(off, tb), :], ext], axis=1)
        rows = jnp.dot(onehot, src, preferred_element_type=F32).astype(BF16)
        keep = acc_s[pl.ds(r0a, BF16_ROWS), :]
        first = jnp.where(head < r0 - r0a, keep, rows[0:BF16_ROWS, :])
        acc_s[pl.ds(r0a, BF16_ROWS), :] = first
        acc_s[pl.ds(r0a + BF16_ROWS, tb), :] = rows[BF16_ROWS:, :]
        return 0

    acc_s[0:BF16_ROWS, :] = jnp.zeros((BF16_ROWS, DX), BF16)
    lax.fori_loop(0, nt, tile, 0, unroll=min(GATHER_UNROLL, nt))
    xe_ref[0] = acc_s[0:cap, :]


def _gather(starts, h2, aff, rank_t, cap):
    B, S, D = h2.shape
    E = rank_t.shape[1]
    tb = min(TOK_TILE, S)
    DX = D + AFF_LANES
    body = functools.partial(_gather_body, S=S, E=E, cap=cap, tb=tb)
    grid_spec = pltpu.PrefetchScalarGridSpec(
        num_scalar_prefetch=1,
        grid=(B, E),
        in_specs=[pl.BlockSpec((1, S, D), lambda b, e, st: (b, 0, 0)),
                  pl.BlockSpec((1, S, E), lambda b, e, st: (b, 0, 0)),
                  pl.BlockSpec((1, 1, 1, S), lambda b, e, st: (b, e, 0, 0))],
        out_specs=pl.BlockSpec((1, cap, DX), lambda b, e, st: (e, b, 0)),
        scratch_shapes=[pltpu.VMEM((cap + tb + BF16_ROWS, DX), BF16)],
    )
    return pl.pallas_call(
        body,
        grid_spec=grid_spec,
        out_shape=jax.ShapeDtypeStruct((E, B * cap, DX), BF16),
        compiler_params=_cparams(("arbitrary", "arbitrary")),
        name="gather",
    )(starts.reshape(-1), h2, aff, rank_t.reshape(B, E, 1, S))


def _ffn_body(x_ref, wg_ref, wu_ref, wd_ref, y_ref, acc_s, wg_s, wu_s, wd_s, *, tt, sub, nfc):
    fc = pl.program_id(2)
    D = y_ref.shape[2]
    wg_s[...] = wg_ref[0].astype(BF16)
    wu_s[...] = wu_ref[0].astype(BF16)
    wd_s[...] = wd_ref[0].astype(BF16)

    for i in range(tt // sub):
        rows = slice(i * sub, (i + 1) * sub)
        x = x_ref[0, rows, 0:D]
        g = jnp.dot(x, wg_s[...], preferred_element_type=F32)
        u = jnp.dot(x, wu_s[...], preferred_element_type=F32)
        a = (g * _sigmoid(g) * u).astype(BF16)
        part = jnp.dot(a, wd_s[...], preferred_element_type=F32)

        def finish(total, rows=rows):
            val = jnp.sum(x_ref[0, rows, D:].astype(F32), axis=1, keepdims=True)
            y_ref[0, rows, :] = (total * val).astype(BF16)

        if nfc == 1:
            finish(part)
        else:
            @pl.when(fc == 0)
            def _(part=part, rows=rows):
                acc_s[rows, :] = part

            @pl.when((fc > 0) & (fc < nfc - 1))
            def _(part=part, rows=rows):
                acc_s[rows, :] += part

            @pl.when(fc == nfc - 1)
            def _(part=part, rows=rows, finish=finish):
                finish(acc_s[rows, :] + part)


def _ffn(xe, w_gate, w_up, w_down):
    E, T, DX = xe.shape
    D = w_gate.shape[1]
    F = w_gate.shape[2]
    tt = min(FFN_ROWS, T)
    tf = min(FFN_HIDDEN, F)
    sub = min(FFN_SUB, tt)
    body = functools.partial(_ffn_body, tt=tt, sub=sub, nfc=F // tf)
    return pl.pallas_call(
        body,
        grid=(E, T // tt, F // tf),
        in_specs=[pl.BlockSpec((1, tt, DX), lambda e, t, f: (e, t, 0)),
                  pl.BlockSpec((1, D, tf), lambda e, t, f: (e, 0, f)),
                  pl.BlockSpec((1, D, tf), lambda e, t, f: (e, 0, f)),
                  pl.BlockSpec((1, tf, D), lambda e, t, f: (e, f, 0))],
        out_specs=pl.BlockSpec((1, tt, D), lambda e, t, f: (e, t, 0)),
        out_shape=jax.ShapeDtypeStruct((E, T, D), BF16),
        scratch_shapes=[pltpu.VMEM((tt, D), F32), pltpu.VMEM((D, tf), BF16), pltpu.VMEM((D, tf), BF16),
                        pltpu.VMEM((tf, D), BF16)],
        compiler_params=_cparams(("arbitrary", "arbitrary", "arbitrary")),
        name="ffn",
    )(xe, w_gate, w_up, w_down)


def _combine_body(starts_ref, y_ref, rank_ref, x1_ref, mod_ref, o_ref, *, S, E, cap, tb):
    b = pl.program_id(0)
    t = pl.program_id(1)
    nt = S // tb
    kw = min(tb, cap)
    lane = lax.broadcasted_iota(I32, (tb, kw), 1)
    rank = rank_ref[0]
    gate = mod_ref[0, 5:6, :]

    def window(e):
        row = (b * E + e) * nt
        r0 = starts_ref[row + t]
        r_end = jnp.where(t == nt - 1, cap, starts_ref[row + jnp.minimum(t + 1, nt - 1)])
        r0a = pl.multiple_of(jnp.minimum((r0 // BF16_ROWS) * BF16_ROWS, cap - kw), BF16_ROWS)
        return r0a, r_end > r0a + kw

    acc = None
    any_over = None
    for e in range(E):
        r0a, over = window(e)
        onehot = (rank[:, e:e + 1] - r0a == lane).astype(F32).astype(BF16)
        part = jnp.dot(onehot, y_ref[e, pl.ds(r0a, kw), :], preferred_element_type=F32)
        acc = part if acc is None else acc + part
        any_over = over if any_over is None else (any_over | over)
    o_ref[0] = x1_ref[0] + gate * acc

    if cap > kw:
        @pl.when(any_over)
        def _():
            for e in range(E):
                r0a, over = window(e)

                @pl.when(over)
                def _(e=e, r0a=r0a):
                    r1a = pl.multiple_of(jnp.minimum(r0a + kw, cap - kw), BF16_ROWS)
                    rk = rank_ref[0, :, e:e + 1]
                    oh = ((rk - r1a == lane) & (rk >= r0a + kw)).astype(F32).astype(BF16)
                    o_ref[0] += gate * jnp.dot(oh, y_ref[e, pl.ds(r1a, kw), :], preferred_element_type=F32)


def _combine(starts, y, rank, x1, mod, cap):
    B, S, D = x1.shape
    E = rank.shape[2]
    tb = min(TOK_TILE, S)
    body = functools.partial(_combine_body, S=S, E=E, cap=cap, tb=tb)
    grid_spec = pltpu.PrefetchScalarGridSpec(
        num_scalar_prefetch=1,
        grid=(B, S // tb),
        in_specs=[pl.BlockSpec((E, cap, D), lambda b, t, st: (0, b, 0)),
                  pl.BlockSpec((1, tb, E), lambda b, t, st: (b, t, 0)),
                  pl.BlockSpec((1, tb, D), lambda b, t, st: (b, t, 0)),
                  pl.BlockSpec((1, 6, D), lambda b, t, st: (b, 0, 0))],
        out_specs=pl.BlockSpec((1, tb, D), lambda b, t, st: (b, t, 0)),
    )
    return pl.pallas_call(
        body,
        grid_spec=grid_spec,
        out_shape=jax.ShapeDtypeStruct((B, S, D), F32),
        compiler_params=_cparams(("arbitrary", "arbitrary")),
        name="combine",
    )(starts.reshape(-1), y, rank, x1, mod)


def _layer(layer_idx, x, c, positions, norm1_g, norm2_g, w_ada, b_ada, w_in, conv_w, gate_b, mnorm_g, qk_g,
           lam_p, subln_g, w_out, w_router, w_gate_e, w_up_e, w_down_e):
    B, S, D = x.shape
    E = w_router.shape[1]
    d_m = mnorm_g.shape[0]
    n_g = 4 * M_HEADS
    d_a = w_out.shape[0] - d_m
    cap = CAP_FACTOR * S // E
    L = min(MLSTM_CHUNK, S)

    mod = _ada(c, w_ada, b_ada).reshape(B, 6, D)

    wm = w_in[:, :4 * d_m].astype(BF16)
    wg = w_in[:, 4 * d_m:4 * d_m + n_g].astype(BF16)
    wa = w_in[:, 4 * d_m + n_g:].astype(BF16)
    pm, pa, gates = _inproj(x, mod, norm1_g.reshape(1, D), wm, wa, wg, gate_b.reshape(1, n_g))

    gt = gates.reshape(B, S // L, L, n_g).transpose(0, 3, 1, 2)
    ym = _mlstm(pm, conv_w, gt, mnorm_g.reshape(1, d_m))

    qn, kn = _qkprep(pa, positions, qk_g)
    ya = _attn(qn, kn, pa, lam_p, subln_g, layer_idx)

    wo = w_out.astype(BF16)
    wr_hi = w_router.astype(BF16)
    wr_lo = (w_router - wr_hi.astype(F32)).astype(BF16)
    x1, h2, aff = _outproj(ym, ya, x, mod, wo[:d_m], wo[d_m:], norm2_g.reshape(1, D), wr_hi, wr_lo)

    rank_t, starts = _topk(aff.transpose(0, 2, 1), cap)
    xe = _gather(starts, h2, aff, rank_t, cap)
    y = _ffn(xe, w_gate_e, w_up_e, w_down_e)
    return _combine(starts, y, rank_t.transpose(0, 2, 1), x1, mod, cap)


def kernel(x, c, positions, norm1_g, norm2_g, w_ada, b_ada, w_in, mlstm_conv_w, mlstm_gate_b, mlstm_norm_g,
           diff_qk_g, diff_lambda, diff_subln_g, w_out, w_router, w_gate_e, w_up_e, w_down_e):
    for l in range(norm1_g.shape[0]):
        x = _layer(l, x, c, positions, norm1_g[l], norm2_g[l], w_ada[l], b_ada[l], w_in[l], mlstm_conv_w[l],
                   mlstm_gate_b[l], mlstm_norm_g[l], diff_qk_g[l], diff_lambda[l], diff_subln_g[l], w_out[l],
                   w_router[l], w_gate_e[l], w_up_e[l], w_down_e[l])
    return x
```

```python
import functools
import math

import numpy as np
import jax
import jax.numpy as jnp
from jax import lax
from jax.experimental import pallas as pl
from jax.experimental.pallas import tpu as pltpu

F32 = jnp.float32
BF16 = jnp.bfloat16
I32 = jnp.int32

M_HEADS = 4
A_HEADS = 4
CONV_K = 5
ROPE_THETA = 500000.0
EPS = 1e-6
CAP_FACTOR = 2

MLSTM_CHUNK = 256
SCAN_UNROLL = 2
ROW_TILE = 512
ATT_TQ = 256
ATT_TK = 512
ATT_GROUPS = 2
TOK_TILE = 256
BF16_ROWS = 16
AFF_LANES = 128
GATHER_UNROLL = 2
GATHER_EXPERTS = 4
SMALL_WIN = 64
FFN_ROWS = 2048
FFN_HIDDEN = 512
FFN_SUB = 512
VMEM_LIMIT = 56 * 1024 * 1024


def _cparams(sem):
    return pltpu.CompilerParams(dimension_semantics=sem, vmem_limit_bytes=VMEM_LIMIT)


def _sigmoid(v):
    return 1.0 / (1.0 + jnp.exp(-v))


def _rms(v):
    return v * lax.rsqrt(jnp.mean(v * v, axis=-1, keepdims=True) + EPS)


def _ada_body(c_ref, w_ref, b_ref, o_ref):
    c = c_ref[...]
    s = (c * _sigmoid(c)).astype(BF16)
    o_ref[...] = jnp.dot(s, w_ref[...].astype(BF16), preferred_element_type=F32) + b_ref[...]


def _ada(c, w_ada, b_ada):
    B, D = c.shape
    N = w_ada.shape[1]
    tn = D
    return pl.pallas_call(
        _ada_body,
        grid=(N // tn,),
        in_specs=[pl.BlockSpec((B, D), lambda j: (0, 0)),
                  pl.BlockSpec((D, tn), lambda j: (0, j)),
                  pl.BlockSpec((1, tn), lambda j: (0, j))],
        out_specs=pl.BlockSpec((B, tn), lambda j: (0, j)),
        out_shape=jax.ShapeDtypeStruct((B, N), F32),
        compiler_params=_cparams(("arbitrary",)),
        name="ada",
    )(c, w_ada, b_ada.reshape(1, N))


def _inproj_body(x_ref, mod_ref, g_ref, wm_ref, wa_ref, wg_ref, bg_ref, pm_ref, pa_ref, gt_ref):
    x = x_ref[0]
    h = _rms(x) * g_ref[...]
    h = (h * (1.0 + mod_ref[0, 1:2, :]) + mod_ref[0, 0:1, :]).astype(BF16)
    pm_ref[0] = jnp.dot(h, wm_ref[...], preferred_element_type=F32).astype(BF16)
    pa_ref[0] = jnp.dot(h, wa_ref[...], preferred_element_type=F32).astype(BF16)
    gt_ref[0] = jnp.dot(h, wg_ref[...], preferred_element_type=F32) + bg_ref[...]


def _inproj(x, mod, g1, wm, wa, wg, bg):
    B, S, D = x.shape
    tm = min(ROW_TILE, S)
    nm, na, ng = wm.shape[1], wa.shape[1], wg.shape[1]
    return pl.pallas_call(
        _inproj_body,
        grid=(B, S // tm),
        in_specs=[pl.BlockSpec((1, tm, D), lambda b, i: (b, i, 0)),
                  pl.BlockSpec((1, 6, D), lambda b, i: (b, 0, 0)),
                  pl.BlockSpec((1, D), lambda b, i: (0, 0)),
                  pl.BlockSpec((D, nm), lambda b, i: (0, 0)),
                  pl.BlockSpec((D, na), lambda b, i: (0, 0)),
                  pl.BlockSpec((D, ng), lambda b, i: (0, 0)),
                  pl.BlockSpec((1, ng), lambda b, i: (0, 0))],
        out_specs=[pl.BlockSpec((1, tm, nm), lambda b, i: (b, i, 0)),
                   pl.BlockSpec((1, tm, na), lambda b, i: (b, i, 0)),
                   pl.BlockSpec((1, tm, ng), lambda b, i: (b, i, 0))],
        out_shape=[jax.ShapeDtypeStruct((B, S, nm), BF16),
                   jax.ShapeDtypeStruct((B, S, na), BF16),
                   jax.ShapeDtypeStruct((B, S, ng), F32)],
        compiler_params=_cparams(("arbitrary", "arbitrary")),
        name="inproj",
    )(x, mod, g1, wm, wa, wg, bg)


def _log_sigmoid(v):
    return jnp.minimum(v, 0.0) - jnp.log(1.0 + jnp.exp(-jnp.abs(v)))


def _mlstm_body(q_ref, k_ref, v_ref, o_ref, cwq_ref, cwk_ref, gt_ref, ng_ref, y_ref,
                xp_s, q_s, kt_s, va_s, hf_s, a_s, *, S, L, dh):
    hd = pl.program_id(1)
    nc = S // L
    win = L + 16

    def conv(src_ref, cw_ref, post):
        xp_s[0:8, :] = jnp.zeros((8, dh), F32)
        xp_s[S + 8:S + 16, :] = jnp.zeros((8, dh), F32)

        def fill(c, _):
            base = pl.multiple_of(c * L, L)
            xp_s[pl.ds(pl.multiple_of(base + 8, 8), L), :] = src_ref[0, pl.ds(base, L), :].astype(F32)
            return 0
        lax.fori_loop(0, nc, fill, 0)

        def body(c, _):
            base = pl.multiple_of(c * L, L)
            w = xp_s[pl.ds(base, win), :]
            acc = None
            for j in range(CONV_K):
                sh = (CONV_K // 2 - j) % win
                r = w if sh == 0 else pltpu.roll(w, sh, 0)
                term = r[8:8 + L, :] * cw_ref[j:j + 1, :]
                acc = term if acc is None else acc + term
            post(base, acc * _sigmoid(acc))
            return 0
        lax.fori_loop(0, nc, body, 0)

    def post_q(base, a):
        q_s[pl.ds(base, L), :] = a.astype(BF16)

    def post_k(base, a):
        kt_s[:, pl.ds(base, L)] = (a * (dh ** -0.5)).T.astype(BF16)

    conv(q_ref, cwq_ref, post_q)
    conv(k_ref, cwk_ref, post_k)

    ones_col = (lax.broadcasted_iota(I32, (L, dh), 1) == 0).astype(BF16)

    def fill_v(c, _):
        base = pl.multiple_of(c * L, L)
        va_s[pl.ds(base, L), 0:dh] = v_ref[0, pl.ds(base, L), :]
        va_s[pl.ds(base, L), dh:2 * dh] = ones_col
        return 0
    lax.fori_loop(0, nc, fill_v, 0)

    ri = lax.broadcasted_iota(I32, (L, L), 0)
    ci = lax.broadcasted_iota(I32, (L, L), 1)

    def scan(reverse, gi, gf, emit):
        logf_all = _log_sigmoid(gt_ref[0, gf])
        tri = ((ri >= ci) if reverse else (ri <= ci)).astype(F32)
        a_s[...] = jnp.dot(logf_all, tri, preferred_element_type=F32, precision=lax.Precision.HIGHEST)
        valid = (ci >= ri) if reverse else (ci <= ri)

        def body(step, carry):
            caug, m = carry
            c = (nc - 1 - step) if reverse else step
            base = pl.multiple_of(c * L, L)
            i_row = gt_ref[0, gi, pl.ds(c, 1), :]
            logf = _log_sigmoid(gt_ref[0, gf, pl.ds(c, 1), :])
            a_row = a_s[pl.ds(c, 1), :]
            a_col = jnp.sum(jnp.where(valid, logf, 0.0), axis=1, keepdims=True)
            g = jnp.sum(logf, axis=1, keepdims=True)
            dmat = jnp.where(valid, a_col - a_row + i_row, -jnp.inf)
            dmax = jnp.max(dmat, axis=1, keepdims=True)
            q_c = q_s[pl.ds(base, L), :]
            kt_c = kt_s[:, pl.ds(base, L)]
            va_c = va_s[pl.ds(base, L), :]
            qk = jnp.dot(q_c, kt_c, preferred_element_type=F32)
            p = (qk * jnp.exp(dmat - dmax)).astype(BF16)
            intra = jnp.dot(p, va_c, preferred_element_type=F32)
            inter = a_col + m
            mj = jnp.maximum(inter, dmax)
            nd = (jnp.exp(dmax - mj) * intra
                  + jnp.exp(inter - mj) * jnp.dot(q_c, caug.astype(BF16), preferred_element_type=F32))
            den = jnp.maximum(jnp.abs(nd[:, dh:dh + 1]), jnp.exp(-mj))
            emit(base, nd[:, 0:dh] * (1.0 / den))
            wlog = g - a_row + i_row
            mloc = jnp.max(wlog, axis=1, keepdims=True)
            kw = (kt_c.astype(F32) * jnp.exp(wlog - mloc)).astype(BF16)
            cloc = jnp.dot(kw, va_c, preferred_element_type=F32)
            m_new = jnp.maximum(g + m, mloc)
            caug = jnp.exp(g + m - m_new) * caug + jnp.exp(mloc - m_new) * cloc
            return caug, m_new

        lax.fori_loop(0, nc, body, (jnp.zeros((dh, 2 * dh), F32), jnp.zeros((1, 1), F32)),
                      unroll=min(SCAN_UNROLL, nc))

    def emit_f(base, h):
        hf_s[pl.ds(base, L), :] = h

    def emit_b(base, h):
        hn = _rms(hf_s[pl.ds(base, L), :] + h) * ng_ref[...]
        og = o_ref[0, pl.ds(base, L), :].astype(F32)
        y_ref[0, pl.ds(base, L), :] = (_sigmoid(og) * hn).astype(BF16)

    scan(False, hd, M_HEADS + hd, emit_f)
    scan(True, 2 * M_HEADS + hd, 3 * M_HEADS + hd, emit_b)


def _mlstm(pm, conv_w, gt, norm_g):
    B, S, _ = pm.shape
    dh = norm_g.shape[1] // M_HEADS
    L = min(MLSTM_CHUNK, S)
    nc = S // L
    H = M_HEADS
    body = functools.partial(_mlstm_body, S=S, L=L, dh=dh)
    return pl.pallas_call(
        body,
        grid=(B, H),
        in_specs=[pl.BlockSpec((1, S, dh), lambda b, h: (b, 0, h)),
                  pl.BlockSpec((1, S, dh), lambda b, h: (b, 0, H + h)),
                  pl.BlockSpec((1, S, dh), lambda b, h: (b, 0, 2 * H + h)),
                  pl.BlockSpec((1, S, dh), lambda b, h: (b, 0, 3 * H + h)),
                  pl.BlockSpec((CONV_K, dh), lambda b, h: (0, h)),
                  pl.BlockSpec((CONV_K, dh), lambda b, h: (0, H + h)),
                  pl.BlockSpec((1, 4 * H, nc, L), lambda b, h: (b, 0, 0, 0)),
                  pl.BlockSpec((1, dh), lambda b, h: (0, h))],
        out_specs=pl.BlockSpec((1, S, dh), lambda b, h: (b, 0, h)),
        out_shape=jax.ShapeDtypeStruct((B, S, H * dh), BF16),
        scratch_shapes=[pltpu.VMEM((S + 16, dh), F32),
                        pltpu.VMEM((S, dh), BF16),
                        pltpu.VMEM((dh, S), BF16),
                        pltpu.VMEM((S, 2 * dh), BF16),
                        pltpu.VMEM((S, dh), F32),
                        pltpu.VMEM((nc, L), F32)],
        compiler_params=_cparams(("arbitrary", "arbitrary")),
        name="mlstm",
    )(pm, pm, pm, pm, conv_w, conv_w, gt, norm_g)


def _qkprep_body(q_ref, k_ref, pos_ref, invf_ref, s1_ref, s2_ref, gm_ref, gq_ref, gk_ref,
                 qo_ref, ko_ref, *, groups, q_scale):
    ang = pos_ref[0].astype(F32) * invf_ref[...]
    cs = jnp.cos(ang)
    sn = jnp.sin(ang)
    s1 = s1_ref[...]
    s2 = s2_ref[...]
    gm = gm_ref[...]
    lanes = gm.shape[0]

    def prep(src_ref, g_ref, dst_ref, scale):
        for j in range(groups):
            x = src_ref[0, :, j * lanes:(j + 1) * lanes].astype(F32)
            ms = jnp.dot((x * x).astype(BF16), gm, preferred_element_type=F32)
            y = x * lax.rsqrt(ms + EPS) * g_ref[...]
            half = pltpu.roll(y, lanes - 8, 1) * s1 + pltpu.roll(y, 8, 1) * s2
            r = y * cs + half * sn
            dst_ref[0, :, j * lanes:(j + 1) * lanes] = (r * scale).astype(BF16)

    prep(q_ref, gq_ref, qo_ref, q_scale)
    prep(k_ref, gk_ref, ko_ref, 1.0)


def _qkprep(pa, positions, qk_g):
    B, S, _ = pa.shape
    hd = qk_g.shape[1]
    rot = hd // 4
    half = rot // 2
    lanes = 128
    W = A_HEADS * 2 * hd
    tm = min(ROW_TILE, S)
    d = np.arange(lanes) % hd
    inv_freq = (ROPE_THETA ** (-np.arange(0, rot, 2, dtype=np.float32) / rot)).astype(np.float32)
    invf = np.where(d < rot, inv_freq[d % half], 0.0).astype(np.float32)[None, :]
    s1 = np.where(d < half, -1.0, 0.0).astype(np.float32)[None, :]
    s2 = np.where((d >= half) & (d < rot), 1.0, 0.0).astype(np.float32)[None, :]
    gm = (np.arange(lanes)[:, None] // hd == np.arange(lanes)[None, :] // hd).astype(np.float32) / hd
    gq = jnp.tile(qk_g[0], lanes // hd)[None, :]
    gk = jnp.tile(qk_g[1], lanes // hd)[None, :]
    q_scale = (hd ** -0.5) * math.log2(math.e)
    body = functools.partial(_qkprep_body, groups=W // lanes, q_scale=q_scale)
    small = lambda shape: pl.BlockSpec(shape, lambda b, i: (0, 0))
    return pl.pallas_call(
        body,
        grid=(B, S // tm),
        in_specs=[pl.BlockSpec((1, tm, W), lambda b, i: (b, i, 0)),
                  pl.BlockSpec((1, tm, W), lambda b, i: (b, i, 1)),
                  pl.BlockSpec((1, tm, 1), lambda b, i: (b, i, 0)),
                  small((1, lanes)), small((1, lanes)), small((1, lanes)), small((lanes, lanes)),
                  small((1, lanes)), small((1, lanes))],
        out_specs=[pl.BlockSpec((1, tm, W), lambda b, i: (b, i, 0)),
                   pl.BlockSpec((1, tm, W), lambda b, i: (b, i, 0))],
        out_shape=[jax.ShapeDtypeStruct((B, S, W), BF16), jax.ShapeDtypeStruct((B, S, W), BF16)],
        compiler_params=_cparams(("arbitrary", "arbitrary")),
        name="qkprep",
    )(pa, pa, positions.reshape(B, S, 1), jnp.asarray(invf), jnp.asarray(s1), jnp.asarray(s2),
      jnp.asarray(gm, dtype=BF16), gq, gk)


def _attn_body(q_ref, k_ref, v_ref, lam_ref, sg_ref, y_ref, va_s, kt_s, s_s, *, S, tq, tk, dv, lam_init):
    nk = S // tk
    lanes = 128

    @pl.when(pl.program_id(2) == 0)
    def _():
        ones_col = (lax.broadcasted_iota(I32, (tk, dv), 1) == 0).astype(BF16)
        for kb in range(nk):
            va_s[kb * tk:(kb + 1) * tk, 0:dv] = v_ref[0, kb * tk:(kb + 1) * tk, :]
            va_s[kb * tk:(kb + 1) * tk, dv:2 * dv] = ones_col
            kt_s[:, kb * tk:(kb + 1) * tk] = k_ref[0, kb * tk:(kb + 1) * tk, :].astype(F32).T.astype(BF16)

    lp = lam_ref[...]
    lam = (jnp.exp(jnp.sum(lp[0:1, :] * lp[1:2, :], axis=1, keepdims=True))
           - jnp.exp(jnp.sum(lp[2:3, :] * lp[3:4, :], axis=1, keepdims=True)) + lam_init)

    q = q_ref[0]
    lane = lax.broadcasted_iota(I32, q.shape, 1)
    hd = q.shape[1] // 2

    qs = jnp.concatenate([jnp.where(lane < hd, q, jnp.zeros_like(q)),
                          jnp.where(lane >= hd, q, jnp.zeros_like(q))], axis=0)
    G = ATT_GROUPS
    rg = 2 * tq // G
    mparts = [None] * G
    for kb in range(nk):
        for g in range(G):
            rows = slice(g * rg, (g + 1) * rg)
            s = jnp.dot(qs[rows, :], kt_s[:, kb * tk:(kb + 1) * tk], preferred_element_type=F32)
            s_s[rows, kb * tk:(kb + 1) * tk] = s
            for j in range(tk // lanes):
                blk = s[:, j * lanes:(j + 1) * lanes]
                mparts[g] = blk if mparts[g] is None else jnp.maximum(mparts[g], blk)
    mbs = [jnp.broadcast_to(jnp.max(mp, axis=1, keepdims=True), (rg, lanes)) for mp in mparts]
    accs = [None] * G
    for kb in range(nk):
        for g in range(G):
            rows = slice(g * rg, (g + 1) * rg)
            p = jnp.concatenate(
                [jnp.exp2(s_s[rows, kb * tk + j * lanes:kb * tk + (j + 1) * lanes] - mbs[g]).astype(BF16)
                 for j in range(tk // lanes)], axis=1)
            part = jnp.dot(p, va_s[kb * tk:(kb + 1) * tk, :], preferred_element_type=F32)
            accs[g] = part if accs[g] is None else accs[g] + part
    acc = jnp.concatenate(accs, axis=0)
    r = acc[:, 0:dv] * (1.0 / acc[:, dv:dv + 1])
    o = r[0:tq, :] - lam * r[tq:2 * tq, :]
    y_ref[0] = (_rms(o) * sg_ref[...] * (1.0 - lam_init)).astype(BF16)


def _attn(qn, kn, pa, lam_p, subln_g, layer_idx):
    B, S, W = qn.shape
    dv = W // A_HEADS
    tq = min(ATT_TQ, S)
    tk = min(ATT_TK, S)
    lam_init = 0.8 - 0.6 * math.exp(-0.3 * layer_idx)
    vcol0 = 2 * W // dv
    body = functools.partial(_attn_body, S=S, tq=tq, tk=tk, dv=dv, lam_init=lam_init)
    return pl.pallas_call(
        body,
        grid=(B, A_HEADS, S // tq),
        in_specs=[pl.BlockSpec((1, tq, dv), lambda b, h, i: (b, i, h)),
                  pl.BlockSpec((1, S, dv), lambda b, h, i: (b, 0, h)),
                  pl.BlockSpec((1, S, dv), lambda b, h, i: (b, 0, vcol0 + h)),
                  pl.BlockSpec(lam_p.shape, lambda b, h, i: (0, 0)),
                  pl.BlockSpec((1, dv), lambda b, h, i: (0, 0))],
        out_specs=pl.BlockSpec((1, tq, dv), lambda b, h, i: (b, i, h)),
        out_shape=jax.ShapeDtypeStruct((B, S, W), BF16),
        scratch_shapes=[pltpu.VMEM((S, 2 * dv), BF16), pltpu.VMEM((dv, S), BF16), pltpu.VMEM((2 * tq, S), F32)],
        compiler_params=_cparams(("arbitrary", "arbitrary", "arbitrary")),
        name="attn",
    )(qn, kn, pa, lam_p, subln_g.reshape(1, dv))


def _outproj_body(ym_ref, ya_ref, x_ref, mod_ref, wom_ref, woa_ref, g2_ref, wrh_ref, wrl_ref,
                  x1_ref, h2_ref, aff_ref):
    mix = (jnp.dot(ym_ref[0], wom_ref[...], preferred_element_type=F32)
           + jnp.dot(ya_ref[0], woa_ref[...], preferred_element_type=F32))
    x1 = x_ref[0] + mod_ref[0, 2:3, :] * mix
    x1_ref[0] = x1
    h2 = _rms(x1) * g2_ref[...] * (1.0 + mod_ref[0, 4:5, :]) + mod_ref[0, 3:4, :]
    hh = h2.astype(BF16)
    h2_ref[0] = hh
    hl = (h2 - hh.astype(F32)).astype(BF16)
    logits = (jnp.dot(hh, wrh_ref[...], preferred_element_type=F32)
              + jnp.dot(hl, wrh_ref[...], preferred_element_type=F32)
              + jnp.dot(hh, wrl_ref[...], preferred_element_type=F32))
    ex = jnp.exp(logits - jnp.max(logits, axis=1, keepdims=True))
    aff_ref[0] = ex * (1.0 / jnp.sum(ex, axis=1, keepdims=True))


def _outproj(ym, ya, x, mod, wom, woa, g2, wrh, wrl):
    B, S, D = x.shape
    tm = min(ROW_TILE, S)
    wm, wa, E = ym.shape[2], ya.shape[2], wrh.shape[1]
    full = lambda shape: pl.BlockSpec(shape, lambda b, i: (0, 0))
    return pl.pallas_call(
        _outproj_body,
        grid=(B, S // tm),
        in_specs=[pl.BlockSpec((1, tm, wm), lambda b, i: (b, i, 0)),
                  pl.BlockSpec((1, tm, wa), lambda b, i: (b, i, 0)),
                  pl.BlockSpec((1, tm, D), lambda b, i: (b, i, 0)),
                  pl.BlockSpec((1, 6, D), lambda b, i: (b, 0, 0)),
                  full((wm, D)), full((wa, D)), full((1, D)), full((D, E)), full((D, E))],
        out_specs=[pl.BlockSpec((1, tm, D), lambda b, i: (b, i, 0)),
                   pl.BlockSpec((1, tm, D), lambda b, i: (b, i, 0)),
                   pl.BlockSpec((1, tm, E), lambda b, i: (b, i, 0))],
        out_shape=[jax.ShapeDtypeStruct((B, S, D), F32),
                   jax.ShapeDtypeStruct((B, S, D), BF16),
                   jax.ShapeDtypeStruct((B, S, E), F32)],
        compiler_params=_cparams(("arbitrary", "arbitrary")),
        name="outproj",
    )(ym, ya, x, mod, wom, woa, g2, wrh, wrl)


def _topk_body(a_ref, rank_ref, starts_ref, *, S, cap, tb):
    bits = lax.bitcast_convert_type(a_ref[0], I32)
    E = bits.shape[0]

    def search(i, thr):
        cand = thr | jnp.left_shift(jnp.int32(1), 30 - i)
        cnt = jnp.sum((bits >= cand).astype(I32), axis=1, keepdims=True)
        return jnp.where(cnt >= cap, cand, thr)
    thr = lax.fori_loop(0, 31, search, jnp.zeros((E, 1), I32))

    gt = bits > thr
    eq = bits == thr
    need = cap - jnp.sum(gt.astype(I32), axis=1, keepdims=True)

    nt = S // tb
    upper = (lax.broadcasted_iota(I32, (tb, tb), 0) < lax.broadcasted_iota(I32, (tb, tb), 1)).astype(BF16)

    def excl_cumsum(mask, with_starts):
        off = jnp.zeros((E, 1), F32)
        outs, starts = [], []
        for t in range(nt):
            blk = mask[:, t * tb:(t + 1) * tb].astype(F32).astype(BF16)
            outs.append(jnp.dot(blk, upper, preferred_element_type=F32) + off)
            starts.append(off)
            off = off + jnp.sum(blk.astype(F32), axis=1, keepdims=True)
        res = jnp.concatenate(outs, axis=1).astype(I32)
        return (res, jnp.concatenate(starts, axis=1).astype(I32)) if with_starts else res

    sel = gt | (eq & (excl_cumsum(eq, False) < need))
    rank, starts = excl_cumsum(sel, True)
    rank_ref[0] = jnp.where(sel, rank, -1)
    starts_ref[0] = starts


def _topk(aff_t, cap):
    B, E, S = aff_t.shape
    tb = min(TOK_TILE, S)
    nt = S // tb
    body = functools.partial(_topk_body, S=S, cap=cap, tb=tb)
    return pl.pallas_call(
        body,
        grid=(B,),
        in_specs=[pl.BlockSpec((1, E, S), lambda b: (b, 0, 0))],
        out_specs=[pl.BlockSpec((1, E, S), lambda b: (b, 0, 0)),
                   pl.BlockSpec((1, E, nt), lambda b: (b, 0, 0))],
        out_shape=[jax.ShapeDtypeStruct((B, E, S), I32), jax.ShapeDtypeStruct((B, E, nt), I32)],
        compiler_params=_cparams(("arbitrary",)),
        name="topk",
    )(aff_t)


def _gather_body(starts_ref, h2_ref, aff_ref, rank_ref, xe_ref, acc_s, *, S, E, EG, cap, tb):
    b = pl.program_id(0)
    g = pl.program_id(1)
    nt = S // tb
    D = h2_ref.shape[2]
    DX = acc_s.shape[2]
    big = tb + BF16_ROWS
    sm = min(SMALL_WIN, big)
    head = lax.broadcasted_iota(I32, (BF16_ROWS, DX), 0)

    def bounds(j, t):
        row = (b * E + g * EG + j) * nt
        r0 = starts_ref[row + t]
        r_end = jnp.where(t == nt - 1, cap, starts_ref[row + jnp.minimum(t + 1, nt - 1)])
        r0a = pl.multiple_of((r0 // BF16_ROWS) * BF16_ROWS, BF16_ROWS)
        return r0, r_end, r0a

    def put(j, r0, r0a, rows, hit, off, w):
        a = jnp.sum(jnp.where(hit, aff_ref[0, j, :, pl.ds(off, tb)], 0.0), axis=1, keepdims=True)
        a1 = a.astype(BF16).astype(F32)
        a2 = (a - a1).astype(BF16).astype(F32)
        a3 = a - a1 - a2
        xlane = lax.broadcasted_iota(I32, (w, AFF_LANES), 1)
        ext = jnp.where(xlane == 0, a1, jnp.where(xlane == 1, a2, jnp.where(xlane == 2, a3, 0.0))).astype(BF16)
        full = jnp.concatenate([rows, ext], axis=1)
        keep = acc_s[j, pl.ds(r0a, BF16_ROWS), :]
        acc_s[j, pl.ds(r0a, BF16_ROWS), :] = jnp.where(head < r0 - r0a, keep, full[0:BF16_ROWS, :])
        acc_s[j, pl.ds(r0a + BF16_ROWS, w - BF16_ROWS), :] = full[BF16_ROWS:, :]

    def tile(t, _):
        off = pl.multiple_of(t * tb, tb)
        bnds = [bounds(j, t) for j in range(EG)]
        fits = None
        for r0, r_end, r0a in bnds:
            f = r_end - r0a < sm
            fits = f if fits is None else (fits & f)

        @pl.when(fits)
        def _():
            sub = lax.broadcasted_iota(I32, (sm, tb), 0)
            hits = [sub == rank_ref[0, j, :, pl.ds(off, tb)] - bnds[j][2] for j in range(EG)]
            onehot = jnp.concatenate([h.astype(F32) for h in hits], axis=0).astype(BF16)
            rows = jnp.dot(onehot, h2_ref[0, pl.ds(off, tb), :], preferred_element_type=F32).astype(BF16)
            for j in range(EG):
                put(j, bnds[j][0], bnds[j][2], rows[j * sm:(j + 1) * sm, :], hits[j], off, sm)

        @pl.when(jnp.logical_not(fits))
        def _():
            sub = lax.broadcasted_iota(I32, (big, tb), 0)
            for j in range(EG):
                hit = sub == rank_ref[0, j, :, pl.ds(off, tb)] - bnds[j][2]
                rows = jnp.dot(hit.astype(F32).astype(BF16), h2_ref[0, pl.ds(off, tb), :],
                               preferred_element_type=F32).astype(BF16)
                put(j, bnds[j][0], bnds[j][2], rows, hit, off, big)
        return 0

    for j in range(EG):
        acc_s[j, 0:BF16_ROWS, :] = jnp.zeros((BF16_ROWS, DX), BF16)
    lax.fori_loop(0, nt, tile, 0, unroll=min(GATHER_UNROLL, nt))
    for j in range(EG):
        xe_ref[j] = acc_s[j, 0:cap, :]


def _gather(starts, h2, aff_t, rank_t, cap):
    B, S, D = h2.shape
    E = rank_t.shape[1]
    EG = min(GATHER_EXPERTS, E)
    tb = min(TOK_TILE, S)
    DX = D + AFF_LANES
    body = functools.partial(_gather_body, S=S, E=E, EG=EG, cap=cap, tb=tb)
    grid_spec = pltpu.PrefetchScalarGridSpec(
        num_scalar_prefetch=1,
        grid=(B, E // EG),
        in_specs=[pl.BlockSpec((1, S, D), lambda b, g, st: (b, 0, 0)),
                  pl.BlockSpec((1, EG, 1, S), lambda b, g, st: (b, g, 0, 0)),
                  pl.BlockSpec((1, EG, 1, S), lambda b, g, st: (b, g, 0, 0))],
        out_specs=pl.BlockSpec((EG, cap, DX), lambda b, g, st: (g, b, 0)),
        scratch_shapes=[pltpu.VMEM((EG, cap + tb + BF16_ROWS, DX), BF16)],
    )
    return pl.pallas_call(
        body,
        grid_spec=grid_spec,
        out_shape=jax.ShapeDtypeStruct((E, B * cap, DX), BF16),
        compiler_params=_cparams(("arbitrary", "arbitrary")),
        name="gather",
    )(starts.reshape(-1), h2, aff_t.reshape(B, E, 1, S), rank_t.reshape(B, E, 1, S))


def _ffn_body(x_ref, wg_ref, wu_ref, wd_ref, y_ref, acc_s, *, tt, sub, nfc):
    fc = pl.program_id(2)
    D = y_ref.shape[2]

    def mm(a, w):
        return lax.dot_general(a, w, (((1,), (0,)), ((), ())), preferred_element_type=F32)

    for i in range(tt // sub):
        rows = slice(i * sub, (i + 1) * sub)
        x = x_ref[0, rows, 0:D]
        g = mm(x, wg_ref[0])
        u = mm(x, wu_ref[0])
        a = (g * _sigmoid(g) * u).astype(BF16)
        part = mm(a, wd_ref[0])

        def finish(total, rows=rows):
            val = jnp.sum(x_ref[0, rows, D:].astype(F32), axis=1, keepdims=True)
            y_ref[0, rows, :] = (total * val).astype(BF16)

        if nfc == 1:
            finish(part)
        else:
            @pl.when(fc == 0)
            def _(part=part, rows=rows):
                acc_s[rows, :] = part

            @pl.when((fc > 0) & (fc < nfc - 1))
            def _(part=part, rows=rows):
                acc_s[rows, :] += part

            @pl.when(fc == nfc - 1)
            def _(part=part, rows=rows, finish=finish):
                finish(acc_s[rows, :] + part)


def _ffn(xe, w_gate, w_up, w_down):
    E, T, DX = xe.shape
    D = w_gate.shape[1]
    F = w_gate.shape[2]
    tt = min(FFN_ROWS, T)
    tf = min(FFN_HIDDEN, F)
    sub = min(FFN_SUB, tt)
    body = functools.partial(_ffn_body, tt=tt, sub=sub, nfc=F // tf)
    return pl.pallas_call(
        body,
        grid=(E, T // tt, F // tf),
        in_specs=[pl.BlockSpec((1, tt, DX), lambda e, t, f: (e, t, 0)),
                  pl.BlockSpec((1, D, tf), lambda e, t, f: (e, 0, f)),
                  pl.BlockSpec((1, D, tf), lambda e, t, f: (e, 0, f)),
                  pl.BlockSpec((1, tf, D), lambda e, t, f: (e, f, 0))],
        out_specs=pl.BlockSpec((1, tt, D), lambda e, t, f: (e, t, 0)),
        out_shape=jax.ShapeDtypeStruct((E, T, D), BF16),
        scratch_shapes=[pltpu.VMEM((tt, D), F32)],
        compiler_params=_cparams(("arbitrary", "arbitrary", "arbitrary")),
        name="ffn",
    )(xe, w_gate, w_up, w_down)


def _combine_body(starts_ref, y_ref, rank_ref, x1_ref, mod_ref, o_ref, ycat_s, *, S, E, cap, tb):
    b = pl.program_id(0)
    t = pl.program_id(1)
    nt = S // tb
    kw = min(tb, cap)
    sm = min(SMALL_WIN, kw)
    lanes = 128
    per = lanes // sm
    rank = rank_ref[0]
    gate = mod_ref[0, 5:6, :]

    def window(e, w):
        row = (b * E + e) * nt
        r0 = starts_ref[row + t]
        r_end = jnp.where(t == nt - 1, cap, starts_ref[row + jnp.minimum(t + 1, nt - 1)])
        r0a = pl.multiple_of(jnp.minimum((r0 // BF16_ROWS) * BF16_ROWS, cap - w), BF16_ROWS)
        return r0a, r_end > r0a + w

    fits = None
    for e in range(E):
        f = jnp.logical_not(window(e, sm)[1])
        fits = f if fits is None else (fits & f)

    @pl.when(fits)
    def _():
        lane = lax.broadcasted_iota(I32, (tb, lanes), 1)
        tiles = []
        for k in range(E // per):
            hit = None
            for i in range(per):
                e = k * per + i
                r0a, _ = window(e, sm)
                ycat_s[e * sm:(e + 1) * sm, :] = y_ref[e, pl.ds(r0a, sm), :]
                col = rank[:, e:e + 1] - r0a
                tgt = jnp.where((col >= 0) & (col < sm), col + i * sm, -1)
                h = lane == tgt
                hit = h if hit is None else (hit | h)
            tiles.append(hit.astype(F32).astype(BF16))
        onehot = jnp.concatenate(tiles, axis=1)
        o_ref[0] = x1_ref[0] + gate * jnp.dot(onehot, ycat_s[...], preferred_element_type=F32)

    @pl.when(jnp.logical_not(fits))
    def _():
        lane = lax.broadcasted_iota(I32, (tb, kw), 1)
        acc = None
        for e in range(E):
            r0a, _ = window(e, kw)
            onehot = (rank[:, e:e + 1] - r0a == lane).astype(F32).astype(BF16)
            part = jnp.dot(onehot, y_ref[e, pl.ds(r0a, kw), :], preferred_element_type=F32)
            acc = part if acc is None else acc + part
        o_ref[0] = x1_ref[0] + gate * acc

        if cap > kw:
            for e in range(E):
                r0a, over = window(e, kw)

                @pl.when(over)
                def _(e=e, r0a=r0a):
                    r1a = pl.multiple_of(jnp.minimum(r0a + kw, cap - kw), BF16_ROWS)
                    rk = rank_ref[0, :, e:e + 1]
                    oh = ((rk - r1a == lane) & (rk >= r0a + kw)).astype(F32).astype(BF16)
                    o_ref[0] += gate * jnp.dot(oh, y_ref[e, pl.ds(r1a, kw), :], preferred_element_type=F32)


def _combine(starts, y, rank, x1, mod, cap):
    B, S, D = x1.shape
    E = rank.shape[2]
    tb = min(TOK_TILE, S)
    body = functools.partial(_combine_body, S=S, E=E, cap=cap, tb=tb)
    grid_spec = pltpu.PrefetchScalarGridSpec(
        num_scalar_prefetch=1,
        grid=(B, S // tb),
        in_specs=[pl.BlockSpec((E, cap, D), lambda b, t, st: (0, b, 0)),
                  pl.BlockSpec((1, tb, E), lambda b, t, st: (b, t, 0)),
                  pl.BlockSpec((1, tb, D), lambda b, t, st: (b, t, 0)),
                  pl.BlockSpec((1, 6, D), lambda b, t, st: (b, 0, 0))],
        out_specs=pl.BlockSpec((1, tb, D), lambda b, t, st: (b, t, 0)),
        scratch_shapes=[pltpu.VMEM((E * min(SMALL_WIN, tb, cap), D), BF16)],
    )
    return pl.pallas_call(
        body,
        grid_spec=grid_spec,
        out_shape=jax.ShapeDtypeStruct((B, S, D), F32),
        compiler_params=_cparams(("arbitrary", "arbitrary")),
        name="combine",
    )(starts.reshape(-1), y, rank, x1, mod)


def _layer(layer_idx, x, c, positions, norm1_g, norm2_g, w_ada, b_ada, w_in, conv_w, gate_b, mnorm_g, qk_g,
           lam_p, subln_g, w_out, w_router, w_gate_e, w_up_e, w_down_e):
    B, S, D = x.shape
    E = w_router.shape[1]
    d_m = mnorm_g.shape[0]
    n_g = 4 * M_HEADS
    d_a = w_out.shape[0] - d_m
    cap = CAP_FACTOR * S // E
    L = min(MLSTM_CHUNK, S)

    mod = _ada(c, w_ada, b_ada).reshape(B, 6, D)

    wm = w_in[:, :4 * d_m].astype(BF16)
    wg = w_in[:, 4 * d_m:4 * d_m + n_g].astype(BF16)
    wa = w_in[:, 4 * d_m + n_g:].astype(BF16)
    pm, pa, gates = _inproj(x, mod, norm1_g.reshape(1, D), wm, wa, wg, gate_b.reshape(1, n_g))

    gt = gates.reshape(B, S // L, L, n_g).transpose(0, 3, 1, 2)
    ym = _mlstm(pm, conv_w, gt, mnorm_g.reshape(1, d_m))

    qn, kn = _qkprep(pa, positions, qk_g)
    ya = _attn(qn, kn, pa, lam_p, subln_g, layer_idx)

    wo = w_out.astype(BF16)
    wr_hi = w_router.astype(BF16)
    wr_lo = (w_router - wr_hi.astype(F32)).astype(BF16)
    x1, h2, aff = _outproj(ym, ya, x, mod, wo[:d_m], wo[d_m:], norm2_g.reshape(1, D), wr_hi, wr_lo)

    aff_t = aff.transpose(0, 2, 1)
    rank_t, starts = _topk(aff_t, cap)
    xe = _gather(starts, h2, aff_t, rank_t, cap)
    y = _ffn(xe, w_gate_e, w_up_e, w_down_e)
    return _combine(starts, y, rank_t.transpose(0, 2, 1), x1, mod, cap)


def kernel(x, c, positions, norm1_g, norm2_g, w_ada, b_ada, w_in, mlstm_conv_w, mlstm_gate_b, mlstm_norm_g,
           diff_qk_g, diff_lambda, diff_subln_g, w_out, w_router, w_gate_e, w_up_e, w_down_e):
    for l in range(norm1_g.shape[0]):
        x = _layer(l, x, c, positions, norm1_g[l], norm2_g[l], w_ada[l], b_ada[l], w_in[l], mlstm_conv_w[l],
                   mlstm_gate_b[l], mlstm_norm_g[l], diff_qk_g[l], diff_lambda[l], diff_subln_g[l], w_out[l],
                   w_router[l], w_gate_e[l], w_up_e[l], w_down_e[l])
    return x
```

```python
import functools
import math

import numpy as np
import jax
import jax.numpy as jnp
from jax import lax
from jax.experimental import pallas as pl
from jax.experimental.pallas import tpu as pltpu

F32 = jnp.float32
BF16 = jnp.bfloat16
I32 = jnp.int32

M_HEADS = 4
A_HEADS = 4
CONV_K = 5
ROPE_THETA = 500000.0
EPS = 1e-6
CAP_FACTOR = 2

MLSTM_CHUNK = 256
SCAN_UNROLL = 4
ROW_TILE = 512
OUTPROJ_TILE = 1024
OUTPROJ_SUB = 256
ATT_TQ = 512
ATT_TK = 512
ATT_GROUPS = 4
TOK_TILE = 256
BF16_ROWS = 16
AFF_LANES = 128
GATHER_UNROLL = 2
GATHER_EXPERTS = 4
SMALL_WIN = 64
FFN_ROWS = 2048
FFN_HIDDEN = 512
FFN_SUB = 512
VMEM_LIMIT = 56 * 1024 * 1024


def _cparams(sem):
    return pltpu.CompilerParams(dimension_semantics=sem, vmem_limit_bytes=VMEM_LIMIT)


def _sigmoid(v):
    return 1.0 / (1.0 + jnp.exp(-v))


def _rms(v):
    return v * lax.rsqrt(jnp.mean(v * v, axis=-1, keepdims=True) + EPS)


def _ada_body(c_ref, w_ref, b_ref, o_ref):
    c = c_ref[...]
    s = (c * _sigmoid(c)).astype(BF16)
    o_ref[...] = jnp.dot(s, w_ref[...].astype(BF16), preferred_element_type=F32) + b_ref[...]


def _ada(c, w_ada, b_ada):
    B, D = c.shape
    N = w_ada.shape[1]
    tn = D
    return pl.pallas_call(
        _ada_body,
        grid=(N // tn,),
        in_specs=[pl.BlockSpec((B, D), lambda j: (0, 0)),
                  pl.BlockSpec((D, tn), lambda j: (0, j)),
                  pl.BlockSpec((1, tn), lambda j: (0, j))],
        out_specs=pl.BlockSpec((B, tn), lambda j: (0, j)),
        out_shape=jax.ShapeDtypeStruct((B, N), F32),
        compiler_params=_cparams(("arbitrary",)),
        name="ada",
    )(c, w_ada, b_ada.reshape(1, N))


def _inproj_body(x_ref, mod_ref, g_ref, wm_ref, wa_ref, wg_ref, bg_ref, pm_ref, pa_ref, gt_ref):
    x = x_ref[0]
    h = _rms(x) * g_ref[...]
    h = (h * (1.0 + mod_ref[0, 1:2, :]) + mod_ref[0, 0:1, :]).astype(BF16)
    pm_ref[0] = jnp.dot(h, wm_ref[...], preferred_element_type=F32).astype(BF16)
    pa_ref[0] = jnp.dot(h, wa_ref[...], preferred_element_type=F32).astype(BF16)
    gt_ref[0] = jnp.dot(h, wg_ref[...], preferred_element_type=F32) + bg_ref[...]


def _inproj(x, mod, g1, wm, wa, wg, bg):
    B, S, D = x.shape
    tm = min(ROW_TILE, S)
    nm, na, ng = wm.shape[1], wa.shape[1], wg.shape[1]
    return pl.pallas_call(
        _inproj_body,
        grid=(B, S // tm),
        in_specs=[pl.BlockSpec((1, tm, D), lambda b, i: (b, i, 0)),
                  pl.BlockSpec((1, 6, D), lambda b, i: (b, 0, 0)),
                  pl.BlockSpec((1, D), lambda b, i: (0, 0)),
                  pl.BlockSpec((D, nm), lambda b, i: (0, 0)),
                  pl.BlockSpec((D, na), lambda b, i: (0, 0)),
                  pl.BlockSpec((D, ng), lambda b, i: (0, 0)),
                  pl.BlockSpec((1, ng), lambda b, i: (0, 0))],
        out_specs=[pl.BlockSpec((1, tm, nm), lambda b, i: (b, i, 0)),
                   pl.BlockSpec((1, tm, na), lambda b, i: (b, i, 0)),
                   pl.BlockSpec((1, tm, ng), lambda b, i: (b, i, 0))],
        out_shape=[jax.ShapeDtypeStruct((B, S, nm), BF16),
                   jax.ShapeDtypeStruct((B, S, na), BF16),
                   jax.ShapeDtypeStruct((B, S, ng), F32)],
        compiler_params=_cparams(("arbitrary", "arbitrary")),
        name="inproj",
    )(x, mod, g1, wm, wa, wg, bg)


def _log_sigmoid(v):
    return jnp.minimum(v, 0.0) - jnp.log(1.0 + jnp.exp(-jnp.abs(v)))


def _mlstm_body(q_ref, k_ref, v_ref, o_ref, cwq_ref, cwk_ref, gt_ref, ng_ref, y_ref,
                xp_s, q_s, kt_s, va_s, hf_s, a_s, *, S, L, dh):
    hd = pl.program_id(1)
    nc = S // L
    win = L + 16

    def conv(src_ref, cw_ref, post):
        xp_s[0:8, :] = jnp.zeros((8, dh), F32)
        xp_s[S + 8:S + 16, :] = jnp.zeros((8, dh), F32)

        def fill(c, _):
            base = pl.multiple_of(c * L, L)
            xp_s[pl.ds(pl.multiple_of(base + 8, 8), L), :] = src_ref[0, pl.ds(base, L), :].astype(F32)
            return 0
        lax.fori_loop(0, nc, fill, 0)

        def body(c, _):
            base = pl.multiple_of(c * L, L)
            w = xp_s[pl.ds(base, win), :]
            acc = None
            for j in range(CONV_K):
                sh = (CONV_K // 2 - j) % win
                r = w if sh == 0 else pltpu.roll(w, sh, 0)
                term = r[8:8 + L, :] * cw_ref[j:j + 1, :]
                acc = term if acc is None else acc + term
            post(base, acc * _sigmoid(acc))
            return 0
        lax.fori_loop(0, nc, body, 0)

    def post_q(base, a):
        q_s[pl.ds(base, L), :] = a.astype(BF16)

    def post_k(base, a):
        kt_s[:, pl.ds(base, L)] = (a * (dh ** -0.5)).T.astype(BF16)

    conv(q_ref, cwq_ref, post_q)
    conv(k_ref, cwk_ref, post_k)

    ones_col = (lax.broadcasted_iota(I32, (L, dh), 1) == 0).astype(BF16)

    def fill_v(c, _):
        base = pl.multiple_of(c * L, L)
        va_s[pl.ds(base, L), 0:dh] = v_ref[0, pl.ds(base, L), :]
        va_s[pl.ds(base, L), dh:2 * dh] = ones_col
        return 0
    lax.fori_loop(0, nc, fill_v, 0)

    ri = lax.broadcasted_iota(I32, (L, L), 0)
    ci = lax.broadcasted_iota(I32, (L, L), 1)

    def scan(reverse, gi, gf, emit):
        logf_all = _log_sigmoid(gt_ref[0, gf])
        tri = ((ri >= ci) if reverse else (ri <= ci)).astype(F32)
        a_s[...] = jnp.dot(logf_all, tri, preferred_element_type=F32, precision=lax.Precision.HIGHEST)
        valid = (ci >= ri) if reverse else (ci <= ri)

        def body(step, carry):
            caug, m = carry
            c = (nc - 1 - step) if reverse else step
            base = pl.multiple_of(c * L, L)
            i_row = gt_ref[0, gi, pl.ds(c, 1), :]
            logf = _log_sigmoid(gt_ref[0, gf, pl.ds(c, 1), :])
            a_row = a_s[pl.ds(c, 1), :]
            a_col = jnp.sum(jnp.where(valid, logf, 0.0), axis=1, keepdims=True)
            g = jnp.sum(logf, axis=1, keepdims=True)
            dmat = jnp.where(valid, a_col - a_row + i_row, -jnp.inf)
            dmax = jnp.max(dmat, axis=1, keepdims=True)
            q_c = q_s[pl.ds(base, L), :]
            kt_c = kt_s[:, pl.ds(base, L)]
            va_c = va_s[pl.ds(base, L), :]
            qk = jnp.dot(q_c, kt_c, preferred_element_type=F32)
            p = (qk * jnp.exp(dmat - dmax)).astype(BF16)
            intra = jnp.dot(p, va_c, preferred_element_type=F32)
            inter = a_col + m
            mj = jnp.maximum(inter, dmax)
            nd = (jnp.exp(dmax - mj) * intra
                  + jnp.exp(inter - mj) * jnp.dot(q_c, caug.astype(BF16), preferred_element_type=F32))
            den = jnp.maximum(jnp.abs(nd[:, dh:dh + 1]), jnp.exp(-mj))
            emit(base, nd[:, 0:dh] * (1.0 / den))
            wlog = g - a_row + i_row
            mloc = jnp.max(wlog, axis=1, keepdims=True)
            kw = (kt_c.astype(F32) * jnp.exp(wlog - mloc)).astype(BF16)
            cloc = jnp.dot(kw, va_c, preferred_element_type=F32)
            m_new = jnp.maximum(g + m, mloc)
            caug = jnp.exp(g + m - m_new) * caug + jnp.exp(mloc - m_new) * cloc
            return caug, m_new

        lax.fori_loop(0, nc, body, (jnp.zeros((dh, 2 * dh), F32), jnp.zeros((1, 1), F32)),
                      unroll=min(SCAN_UNROLL, nc))

    def emit_f(base, h):
        hf_s[pl.ds(base, L), :] = h

    def emit_b(base, h):
        hn = _rms(hf_s[pl.ds(base, L), :] + h) * ng_ref[...]
        og = o_ref[0, pl.ds(base, L), :].astype(F32)
        y_ref[0, pl.ds(base, L), :] = (_sigmoid(og) * hn).astype(BF16)

    scan(False, hd, M_HEADS + hd, emit_f)
    scan(True, 2 * M_HEADS + hd, 3 * M_HEADS + hd, emit_b)


def _mlstm(pm, conv_w, gt, norm_g):
    B, S, _ = pm.shape
    dh = norm_g.shape[1] // M_HEADS
    L = min(MLSTM_CHUNK, S)
    nc = S // L
    H = M_HEADS
    body = functools.partial(_mlstm_body, S=S, L=L, dh=dh)
    return pl.pallas_call(
        body,
        grid=(B, H),
        in_specs=[pl.BlockSpec((1, S, dh), lambda b, h: (b, 0, h)),
                  pl.BlockSpec((1, S, dh), lambda b, h: (b, 0, H + h)),
                  pl.BlockSpec((1, S, dh), lambda b, h: (b, 0, 2 * H + h)),
                  pl.BlockSpec((1, S, dh), lambda b, h: (b, 0, 3 * H + h)),
                  pl.BlockSpec((CONV_K, dh), lambda b, h: (0, h)),
                  pl.BlockSpec((CONV_K, dh), lambda b, h: (0, H + h)),
                  pl.BlockSpec((1, 4 * H, nc, L), lambda b, h: (b, 0, 0, 0)),
                  pl.BlockSpec((1, dh), lambda b, h: (0, h))],
        out_specs=pl.BlockSpec((1, S, dh), lambda b, h: (b, 0, h)),
        out_shape=jax.ShapeDtypeStruct((B, S, H * dh), BF16),
        scratch_shapes=[pltpu.VMEM((S + 16, dh), F32),
                        pltpu.VMEM((S, dh), BF16),
                        pltpu.VMEM((dh, S), BF16),
                        pltpu.VMEM((S, 2 * dh), BF16),
                        pltpu.VMEM((S, dh), F32),
                        pltpu.VMEM((nc, L), F32)],
        compiler_params=_cparams(("arbitrary", "arbitrary")),
        name="mlstm",
    )(pm, pm, pm, pm, conv_w, conv_w, gt, norm_g)


def _qkprep_body(q_ref, k_ref, pos_ref, invf_ref, s1_ref, s2_ref, gm_ref, gq_ref, gk_ref,
                 qo_ref, ko_ref, *, groups, q_scale):
    ang = pos_ref[0].astype(F32) * invf_ref[...]
    cs = jnp.cos(ang)
    sn = jnp.sin(ang)
    s1 = s1_ref[...]
    s2 = s2_ref[...]
    gm = gm_ref[...]
    lanes = gm.shape[0]

    def prep(src_ref, g_ref, dst_ref, scale):
        for j in range(groups):
            x = src_ref[0, :, j * lanes:(j + 1) * lanes].astype(F32)
            ms = jnp.dot((x * x).astype(BF16), gm, preferred_element_type=F32)
            y = x * lax.rsqrt(ms + EPS) * g_ref[...]
            half = pltpu.roll(y, lanes - 8, 1) * s1 + pltpu.roll(y, 8, 1) * s2
            r = y * cs + half * sn
            dst_ref[0, :, j * lanes:(j + 1) * lanes] = (r * scale).astype(BF16)

    prep(q_ref, gq_ref, qo_ref, q_scale)
    prep(k_ref, gk_ref, ko_ref, 1.0)


def _qkprep(pa, positions, qk_g):
    B, S, _ = pa.shape
    hd = qk_g.shape[1]
    rot = hd // 4
    half = rot // 2
    lanes = 128
    W = A_HEADS * 2 * hd
    tm = min(ROW_TILE, S)
    d = np.arange(lanes) % hd
    inv_freq = (ROPE_THETA ** (-np.arange(0, rot, 2, dtype=np.float32) / rot)).astype(np.float32)
    invf = np.where(d < rot, inv_freq[d % half], 0.0).astype(np.float32)[None, :]
    s1 = np.where(d < half, -1.0, 0.0).astype(np.float32)[None, :]
    s2 = np.where((d >= half) & (d < rot), 1.0, 0.0).astype(np.float32)[None, :]
    gm = (np.arange(lanes)[:, None] // hd == np.arange(lanes)[None, :] // hd).astype(np.float32) / hd
    gq = jnp.tile(qk_g[0], lanes // hd)[None, :]
    gk = jnp.tile(qk_g[1], lanes // hd)[None, :]
    q_scale = (hd ** -0.5) * math.log2(math.e)
    body = functools.partial(_qkprep_body, groups=W // lanes, q_scale=q_scale)
    small = lambda shape: pl.BlockSpec(shape, lambda b, i: (0, 0))
    return pl.pallas_call(
        body,
        grid=(B, S // tm),
        in_specs=[pl.BlockSpec((1, tm, W), lambda b, i: (b, i, 0)),
                  pl.BlockSpec((1, tm, W), lambda b, i: (b, i, 1)),
                  pl.BlockSpec((1, tm, 1), lambda b, i: (b, i, 0)),
                  small((1, lanes)), small((1, lanes)), small((1, lanes)), small((lanes, lanes)),
                  small((1, lanes)), small((1, lanes))],
        out_specs=[pl.BlockSpec((1, tm, W), lambda b, i: (b, i, 0)),
                   pl.BlockSpec((1, tm, W), lambda b, i: (b, i, 0))],
        out_shape=[jax.ShapeDtypeStruct((B, S, W), BF16), jax.ShapeDtypeStruct((B, S, W), BF16)],
        compiler_params=_cparams(("arbitrary", "arbitrary")),
        name="qkprep",
    )(pa, pa, positions.reshape(B, S, 1), jnp.asarray(invf), jnp.asarray(s1), jnp.asarray(s2),
      jnp.asarray(gm, dtype=BF16), gq, gk)


def _attn_body(q_ref, k_ref, v_ref, lam_ref, sg_ref, y_ref, va_s, kt_s, s_s, *, S, tq, tk, dv, lam_init):
    nk = S // tk
    lanes = 128

    @pl.when(pl.program_id(2) == 0)
    def _():
        ones_col = (lax.broadcasted_iota(I32, (tk, dv), 1) == 0).astype(BF16)
        for kb in range(nk):
            va_s[kb * tk:(kb + 1) * tk, 0:dv] = v_ref[0, kb * tk:(kb + 1) * tk, :]
            va_s[kb * tk:(kb + 1) * tk, dv:2 * dv] = ones_col
            kt_s[:, kb * tk:(kb + 1) * tk] = k_ref[0, kb * tk:(kb + 1) * tk, :].astype(F32).T.astype(BF16)

    lp = lam_ref[...]
    lam = (jnp.exp(jnp.sum(lp[0:1, :] * lp[1:2, :], axis=1, keepdims=True))
           - jnp.exp(jnp.sum(lp[2:3, :] * lp[3:4, :], axis=1, keepdims=True)) + lam_init)

    q = q_ref[0]
    lane = lax.broadcasted_iota(I32, q.shape, 1)
    hd = q.shape[1] // 2

    qs = jnp.concatenate([jnp.where(lane < hd, q, jnp.zeros_like(q)),
                          jnp.where(lane >= hd, q, jnp.zeros_like(q))], axis=0)
    G = ATT_GROUPS
    rg = 2 * tq // G
    mparts, mbs, accs = [None] * G, [None] * G, [None] * G

    def score(g, kb):
        rows = slice(g * rg, (g + 1) * rg)
        s = jnp.dot(qs[rows, :], kt_s[:, kb * tk:(kb + 1) * tk], preferred_element_type=F32)
        s_s[rows, kb * tk:(kb + 1) * tk] = s
        for j in range(tk // lanes):
            blk = s[:, j * lanes:(j + 1) * lanes]
            mparts[g] = blk if mparts[g] is None else jnp.maximum(mparts[g], blk)

    def rowmax(g):
        mbs[g] = jnp.broadcast_to(jnp.max(mparts[g], axis=1, keepdims=True), (rg, lanes))

    def weigh(g, kb):
        rows = slice(g * rg, (g + 1) * rg)
        p = jnp.concatenate(
            [jnp.exp2(s_s[rows, kb * tk + j * lanes:kb * tk + (j + 1) * lanes] - mbs[g]).astype(BF16)
             for j in range(tk // lanes)], axis=1)
        part = jnp.dot(p, va_s[kb * tk:(kb + 1) * tk, :], preferred_element_type=F32)
        accs[g] = part if accs[g] is None else accs[g] + part

    first, second = range(G // 2), range(G // 2, G)
    for kb in range(nk):
        for g in first:
            score(g, kb)
    for g in first:
        rowmax(g)
    for kb in range(nk):
        for g in second:
            score(g, kb)
        for g in first:
            weigh(g, kb)
    for g in second:
        rowmax(g)
    for kb in range(nk):
        for g in second:
            weigh(g, kb)
    acc = jnp.concatenate(accs, axis=0)
    r = acc[:, 0:dv] * (1.0 / acc[:, dv:dv + 1])
    o = r[0:tq, :] - lam * r[tq:2 * tq, :]
    y_ref[0] = (_rms(o) * sg_ref[...] * (1.0 - lam_init)).astype(BF16)


def _attn(qn, kn, pa, lam_p, subln_g, layer_idx):
    B, S, W = qn.shape
    dv = W // A_HEADS
    tq = min(ATT_TQ, S)
    tk = min(ATT_TK, S)
    lam_init = 0.8 - 0.6 * math.exp(-0.3 * layer_idx)
    vcol0 = 2 * W // dv
    body = functools.partial(_attn_body, S=S, tq=tq, tk=tk, dv=dv, lam_init=lam_init)
    return pl.pallas_call(
        body,
        grid=(B, A_HEADS, S // tq),
        in_specs=[pl.BlockSpec((1, tq, dv), lambda b, h, i: (b, i, h)),
                  pl.BlockSpec((1, S, dv), lambda b, h, i: (b, 0, h)),
                  pl.BlockSpec((1, S, dv), lambda b, h, i: (b, 0, vcol0 + h)),
                  pl.BlockSpec(lam_p.shape, lambda b, h, i: (0, 0)),
                  pl.BlockSpec((1, dv), lambda b, h, i: (0, 0))],
        out_specs=pl.BlockSpec((1, tq, dv), lambda b, h, i: (b, i, h)),
        out_shape=jax.ShapeDtypeStruct((B, S, W), BF16),
        scratch_shapes=[pltpu.VMEM((S, 2 * dv), BF16), pltpu.VMEM((dv, S), BF16), pltpu.VMEM((2 * tq, S), F32)],
        compiler_params=_cparams(("arbitrary", "arbitrary", "arbitrary")),
        name="attn",
    )(qn, kn, pa, lam_p, subln_g.reshape(1, dv))


def _outproj_body(ym_ref, ya_ref, x_ref, mod_ref, wom_ref, woa_ref, g2_ref, wrh_ref, wrl_ref,
                  x1_ref, h2_ref, aff_ref, *, sub):
    for r in range(x_ref.shape[1] // sub):
        rows = slice(r * sub, (r + 1) * sub)
        mix = (jnp.dot(ym_ref[0, rows, :], wom_ref[...], preferred_element_type=F32)
               + jnp.dot(ya_ref[0, rows, :], woa_ref[...], preferred_element_type=F32))
        x1 = x_ref[0, rows, :] + mod_ref[0, 2:3, :] * mix
        x1_ref[0, rows, :] = x1
        h2 = _rms(x1) * g2_ref[...] * (1.0 + mod_ref[0, 4:5, :]) + mod_ref[0, 3:4, :]
        hh = h2.astype(BF16)
        h2_ref[0, rows, :] = hh
        hl = (h2 - hh.astype(F32)).astype(BF16)
        logits = (jnp.dot(hh, wrh_ref[...], preferred_element_type=F32)
                  + jnp.dot(hl, wrh_ref[...], preferred_element_type=F32)
                  + jnp.dot(hh, wrl_ref[...], preferred_element_type=F32))
        ex = jnp.exp(logits - jnp.max(logits, axis=1, keepdims=True))
        aff_ref[0, rows, :] = ex * (1.0 / jnp.sum(ex, axis=1, keepdims=True))


def _outproj(ym, ya, x, mod, wom, woa, g2, wrh, wrl):
    B, S, D = x.shape
    tm = min(OUTPROJ_TILE, S)
    wm, wa, E = ym.shape[2], ya.shape[2], wrh.shape[1]
    full = lambda shape: pl.BlockSpec(shape, lambda b, i: (0, 0))
    return pl.pallas_call(
        functools.partial(_outproj_body, sub=min(OUTPROJ_SUB, tm)),
        grid=(B, S // tm),
        in_specs=[pl.BlockSpec((1, tm, wm), lambda b, i: (b, i, 0)),
                  pl.BlockSpec((1, tm, wa), lambda b, i: (b, i, 0)),
                  pl.BlockSpec((1, tm, D), lambda b, i: (b, i, 0)),
                  pl.BlockSpec((1, 6, D), lambda b, i: (b, 0, 0)),
                  full((wm, D)), full((wa, D)), full((1, D)), full((D, E)), full((D, E))],
        out_specs=[pl.BlockSpec((1, tm, D), lambda b, i: (b, i, 0)),
                   pl.BlockSpec((1, tm, D), lambda b, i: (b, i, 0)),
                   pl.BlockSpec((1, tm, E), lambda b, i: (b, i, 0))],
        out_shape=[jax.ShapeDtypeStruct((B, S, D), F32),
                   jax.ShapeDtypeStruct((B, S, D), BF16),
                   jax.ShapeDtypeStruct((B, S, E), F32)],
        compiler_params=_cparams(("arbitrary", "arbitrary")),
        name="outproj",
    )(ym, ya, x, mod, wom, woa, g2, wrh, wrl)


def _topk_body(a_ref, rank_ref, starts_ref, *, S, cap, tb):
    bits = lax.bitcast_convert_type(a_ref[0], I32)
    E = bits.shape[0]

    def search(i, thr):
        cand = thr | jnp.left_shift(jnp.int32(1), 30 - i)
        cnt = jnp.sum((bits >= cand).astype(I32), axis=1, keepdims=True)
        return jnp.where(cnt >= cap, cand, thr)
    thr = lax.fori_loop(0, 31, search, jnp.zeros((E, 1), I32))

    gt = bits > thr
    eq = bits == thr
    need = cap - jnp.sum(gt.astype(I32), axis=1, keepdims=True)

    nt = S // tb
    upper = (lax.broadcasted_iota(I32, (tb, tb), 0) < lax.broadcasted_iota(I32, (tb, tb), 1)).astype(BF16)

    def excl_cumsum(mask, with_starts):
        off = jnp.zeros((E, 1), F32)
        outs, starts = [], []
        for t in range(nt):
            blk = mask[:, t * tb:(t + 1) * tb].astype(F32).astype(BF16)
            outs.append(jnp.dot(blk, upper, preferred_element_type=F32) + off)
            starts.append(off)
            off = off + jnp.sum(blk.astype(F32), axis=1, keepdims=True)
        res = jnp.concatenate(outs, axis=1).astype(I32)
        return (res, jnp.concatenate(starts, axis=1).astype(I32)) if with_starts else res

    sel = gt | (eq & (excl_cumsum(eq, False) < need))
    rank, starts = excl_cumsum(sel, True)
    rank_ref[0] = jnp.where(sel, rank, -1)
    starts_ref[0] = starts


def _topk(aff_t, cap):
    B, E, S = aff_t.shape
    tb = min(TOK_TILE, S)
    nt = S // tb
    body = functools.partial(_topk_body, S=S, cap=cap, tb=tb)
    return pl.pallas_call(
        body,
        grid=(B,),
        in_specs=[pl.BlockSpec((1, E, S), lambda b: (b, 0, 0))],
        out_specs=[pl.BlockSpec((1, E, S), lambda b: (b, 0, 0)),
                   pl.BlockSpec((1, E, nt), lambda b: (b, 0, 0))],
        out_shape=[jax.ShapeDtypeStruct((B, E, S), I32), jax.ShapeDtypeStruct((B, E, nt), I32)],
        compiler_params=_cparams(("arbitrary",)),
        name="topk",
    )(aff_t)


def _gather_body(starts_ref, h2_ref, aff_ref, rank_ref, xe_ref, acc_s, *, S, E, EG, cap, tb):
    b = pl.program_id(0)
    g = pl.program_id(1)
    nt = S // tb
    D = h2_ref.shape[2]
    DX = acc_s.shape[2]
    big = tb + BF16_ROWS
    sm = min(SMALL_WIN, big)
    head = lax.broadcasted_iota(I32, (BF16_ROWS, DX), 0)

    def bounds(j, t):
        row = (b * E + g * EG + j) * nt
        r0 = starts_ref[row + t]
        r_end = jnp.where(t == nt - 1, cap, starts_ref[row + jnp.minimum(t + 1, nt - 1)])
        r0a = pl.multiple_of((r0 // BF16_ROWS) * BF16_ROWS, BF16_ROWS)
        return r0, r_end, r0a

    def put(j, r0, r0a, rows, hit, off, w):
        a = jnp.sum(jnp.where(hit, aff_ref[0, j, :, pl.ds(off, tb)], 0.0), axis=1, keepdims=True)
        a1 = a.astype(BF16).astype(F32)
        a2 = (a - a1).astype(BF16).astype(F32)
        a3 = a - a1 - a2
        xlane = lax.broadcasted_iota(I32, (w, AFF_LANES), 1)
        ext = jnp.where(xlane == 0, a1, jnp.where(xlane == 1, a2, jnp.where(xlane == 2, a3, 0.0))).astype(BF16)
        full = jnp.concatenate([rows, ext], axis=1)
        keep = acc_s[j, pl.ds(r0a, BF16_ROWS), :]
        acc_s[j, pl.ds(r0a, BF16_ROWS), :] = jnp.where(head < r0 - r0a, keep, full[0:BF16_ROWS, :])
        acc_s[j, pl.ds(r0a + BF16_ROWS, w - BF16_ROWS), :] = full[BF16_ROWS:, :]

    def tile(t, _):
        off = pl.multiple_of(t * tb, tb)
        bnds = [bounds(j, t) for j in range(EG)]
        fits = None
        for r0, r_end, r0a in bnds:
            f = r_end - r0a < sm
            fits = f if fits is None else (fits & f)

        @pl.when(fits)
        def _():
            sub = lax.broadcasted_iota(I32, (sm, tb), 0)
            hits = [sub == rank_ref[0, j, :, pl.ds(off, tb)] - bnds[j][2] for j in range(EG)]
            onehot = jnp.concatenate([h.astype(F32) for h in hits], axis=0).astype(BF16)
            rows = jnp.dot(onehot, h2_ref[0, pl.ds(off, tb), :], preferred_element_type=F32).astype(BF16)
            for j in range(EG):
                put(j, bnds[j][0], bnds[j][2], rows[j * sm:(j + 1) * sm, :], hits[j], off, sm)

        @pl.when(jnp.logical_not(fits))
        def _():
            sub = lax.broadcasted_iota(I32, (big, tb), 0)
            for j in range(EG):
                hit = sub == rank_ref[0, j, :, pl.ds(off, tb)] - bnds[j][2]
                rows = jnp.dot(hit.astype(F32).astype(BF16), h2_ref[0, pl.ds(off, tb), :],
                               preferred_element_type=F32).astype(BF16)
                put(j, bnds[j][0], bnds[j][2], rows, hit, off, big)
        return 0

    for j in range(EG):
        acc_s[j, 0:BF16_ROWS, :] = jnp.zeros((BF16_ROWS, DX), BF16)
    lax.fori_loop(0, nt, tile, 0, unroll=min(GATHER_UNROLL, nt))
    for j in range(EG):
        xe_ref[j] = acc_s[j, 0:cap, :]


def _gather(starts, h2, aff_t, rank_t, cap):
    B, S, D = h2.shape
    E = rank_t.shape[1]
    EG = min(GATHER_EXPERTS, E)
    tb = min(TOK_TILE, S)
    DX = D + AFF_LANES
    body = functools.partial(_gather_body, S=S, E=E, EG=EG, cap=cap, tb=tb)
    grid_spec = pltpu.PrefetchScalarGridSpec(
        num_scalar_prefetch=1,
        grid=(B, E // EG),
        in_specs=[pl.BlockSpec((1, S, D), lambda b, g, st: (b, 0, 0)),
                  pl.BlockSpec((1, EG, 1, S), lambda b, g, st: (b, g, 0, 0)),
                  pl.BlockSpec((1, EG, 1, S), lambda b, g, st: (b, g, 0, 0))],
        out_specs=pl.BlockSpec((EG, cap, DX), lambda b, g, st: (g, b, 0)),
        scratch_shapes=[pltpu.VMEM((EG, cap + tb + BF16_ROWS, DX), BF16)],
    )
    return pl.pallas_call(
        body,
        grid_spec=grid_spec,
        out_shape=jax.ShapeDtypeStruct((E, B * cap, DX), BF16),
        compiler_params=_cparams(("arbitrary", "arbitrary")),
        name="gather",
    )(starts.reshape(-1), h2, aff_t.reshape(B, E, 1, S), rank_t.reshape(B, E, 1, S))


def _ffn_body(x_ref, wg_ref, wu_ref, wd_ref, y_ref, acc_s, *, tt, sub, nfc):
    fc = pl.program_id(2)
    D = y_ref.shape[2]

    def mm(a, w):
        return lax.dot_general(a, w, (((1,), (0,)), ((), ())), preferred_element_type=F32)

    if nfc > 1:
        @pl.when((pl.program_id(0) == 0) & (pl.program_id(1) == 0) & (fc == 0))
        def _():
            acc_s[...] = jnp.zeros(acc_s.shape, F32)

    for i in range(tt // sub):
        rows = slice(i * sub, (i + 1) * sub)
        x = x_ref[0, rows, 0:D]
        g = mm(x, wg_ref[0])
        u = mm(x, wu_ref[0])
        a = (g * _sigmoid(g) * u).astype(BF16)
        total = mm(a, wd_ref[0])
        if nfc > 1:
            total = jnp.where(fc == 0, total, acc_s[rows, :] + total)
            acc_s[rows, :] = total
        val = jnp.sum(x_ref[0, rows, D:].astype(F32), axis=1, keepdims=True)
        y_ref[0, rows, :] = (total * val).astype(BF16)


def _ffn(xe, w_gate, w_up, w_down):
    E, T, DX = xe.shape
    D = w_gate.shape[1]
    F = w_gate.shape[2]
    tt = min(FFN_ROWS, T)
    tf = min(FFN_HIDDEN, F)
    sub = min(FFN_SUB, tt)
    body = functools.partial(_ffn_body, tt=tt, sub=sub, nfc=F // tf)
    return pl.pallas_call(
        body,
        grid=(E, T // tt, F // tf),
        in_specs=[pl.BlockSpec((1, tt, DX), lambda e, t, f: (e, t, 0)),
                  pl.BlockSpec((1, D, tf), lambda e, t, f: (e, 0, f)),
                  pl.BlockSpec((1, D, tf), lambda e, t, f: (e, 0, f)),
                  pl.BlockSpec((1, tf, D), lambda e, t, f: (e, f, 0))],
        out_specs=pl.BlockSpec((1, tt, D), lambda e, t, f: (e, t, 0)),
        out_shape=jax.ShapeDtypeStruct((E, T, D), BF16),
        scratch_shapes=[pltpu.VMEM((tt, D), F32)],
        compiler_params=_cparams(("arbitrary", "arbitrary", "arbitrary")),
        name="ffn",
    )(xe, w_gate, w_up, w_down)


def _combine_body(starts_ref, y_ref, rank_ref, x1_ref, mod_ref, o_ref, ycat_s, *, S, E, cap, tb):
    b = pl.program_id(0)
    t = pl.program_id(1)
    nt = S // tb
    kw = min(tb, cap)
    sm = min(SMALL_WIN, kw)
    lanes = 128
    per = lanes // sm
    rank = rank_ref[0]
    gate = mod_ref[0, 5:6, :]

    def window(e, w):
        row = (b * E + e) * nt
        r0 = starts_ref[row + t]
        r_end = jnp.where(t == nt - 1, cap, starts_ref[row + jnp.minimum(t + 1, nt - 1)])
        r0a = pl.multiple_of(jnp.minimum((r0 // BF16_ROWS) * BF16_ROWS, cap - w), BF16_ROWS)
        return r0a, r_end > r0a + w

    fits = None
    for e in range(E):
        f = jnp.logical_not(window(e, sm)[1])
        fits = f if fits is None else (fits & f)

    @pl.when(fits)
    def _():
        lane = lax.broadcasted_iota(I32, (tb, lanes), 1)
        tiles = []
        for k in range(E // per):
            hit = None
            for i in range(per):
                e = k * per + i
                r0a, _ = window(e, sm)
                ycat_s[e * sm:(e + 1) * sm, :] = y_ref[e, pl.ds(r0a, sm), :]
                col = rank[:, e:e + 1] - r0a
                tgt = jnp.where((col >= 0) & (col < sm), col + i * sm, -1)
                h = lane == tgt
                hit = h if hit is None else (hit | h)
            tiles.append(hit.astype(F32).astype(BF16))
        onehot = jnp.concatenate(tiles, axis=1)
        o_ref[0] = x1_ref[0] + gate * jnp.dot(onehot, ycat_s[...], preferred_element_type=F32)

    @pl.when(jnp.logical_not(fits))
    def _():
        lane = lax.broadcasted_iota(I32, (tb, kw), 1)
        acc = None
        for e in range(E):
            r0a, _ = window(e, kw)
            onehot = (rank[:, e:e + 1] - r0a == lane).astype(F32).astype(BF16)
            part = jnp.dot(onehot, y_ref[e, pl.ds(r0a, kw), :], preferred_element_type=F32)
            acc = part if acc is None else acc + part
        o_ref[0] = x1_ref[0] + gate * acc

        if cap > kw:
            for e in range(E):
                r0a, over = window(e, kw)

                @pl.when(over)
                def _(e=e, r0a=r0a):
                    r1a = pl.multiple_of(jnp.minimum(r0a + kw, cap - kw), BF16_ROWS)
                    rk = rank_ref[0, :, e:e + 1]
                    oh = ((rk - r1a == lane) & (rk >= r0a + kw)).astype(F32).astype(BF16)
                    o_ref[0] += gate * jnp.dot(oh, y_ref[e, pl.ds(r1a, kw), :], preferred_element_type=F32)


def _combine(starts, y, rank, x1, mod, cap):
    B, S, D = x1.shape
    E = rank.shape[2]
    tb = min(TOK_TILE, S)
    body = functools.partial(_combine_body, S=S, E=E, cap=cap, tb=tb)
    grid_spec = pltpu.PrefetchScalarGridSpec(
        num_scalar_prefetch=1,
        grid=(B, S // tb),
        in_specs=[pl.BlockSpec((E, cap, D), lambda b, t, st: (0, b, 0)),
                  pl.BlockSpec((1, tb, E), lambda b, t, st: (b, t, 0)),
                  pl.BlockSpec((1, tb, D), lambda b, t, st: (b, t, 0)),
                  pl.BlockSpec((1, 6, D), lambda b, t, st: (b, 0, 0))],
        out_specs=pl.BlockSpec((1, tb, D), lambda b, t, st: (b, t, 0)),
        scratch_shapes=[pltpu.VMEM((E * min(SMALL_WIN, tb, cap), D), BF16)],
    )
    return pl.pallas_call(
        body,
        grid_spec=grid_spec,
        out_shape=jax.ShapeDtypeStruct((B, S, D), F32),
        compiler_params=_cparams(("arbitrary", "arbitrary")),
        name="combine",
    )(starts.reshape(-1), y, rank, x1, mod)


def _layer(layer_idx, x, c, positions, norm1_g, norm2_g, w_ada, b_ada, w_in, conv_w, gate_b, mnorm_g, qk_g,
           lam_p, subln_g, w_out, w_router, w_gate_e, w_up_e, w_down_e):
    B, S, D = x.shape
    E = w_router.shape[1]
    d_m = mnorm_g.shape[0]
    n_g = 4 * M_HEADS
    d_a = w_out.shape[0] - d_m
    cap = CAP_FACTOR * S // E
    L = min(MLSTM_CHUNK, S)

    mod = _ada(c, w_ada, b_ada).reshape(B, 6, D)

    wm = w_in[:, :4 * d_m].astype(BF16)
    wg = w_in[:, 4 * d_m:4 * d_m + n_g].astype(BF16)
    wa = w_in[:, 4 * d_m + n_g:].astype(BF16)
    pm, pa, gates = _inproj(x, mod, norm1_g.reshape(1, D), wm, wa, wg, gate_b.reshape(1, n_g))

    gt = gates.reshape(B, S // L, L, n_g).transpose(0, 3, 1, 2)
    ym = _mlstm(pm, conv_w, gt, mnorm_g.reshape(1, d_m))

    qn, kn = _qkprep(pa, positions, qk_g)
    ya = _attn(qn, kn, pa, lam_p, subln_g, layer_idx)

    wo = w_out.astype(BF16)
    wr_hi = w_router.astype(BF16)
    wr_lo = (w_router - wr_hi.astype(F32)).astype(BF16)
    x1, h2, aff = _outproj(ym, ya, x, mod, wo[:d_m], wo[d_m:], norm2_g.reshape(1, D), wr_hi, wr_lo)

    aff_t = aff.transpose(0, 2, 1)
    rank_t, starts = _topk(aff_t, cap)
    xe = _gather(starts, h2, aff_t, rank_t, cap)
    y = _ffn(xe, w_gate_e, w_up_e, w_down_e)
    return _combine(starts, y, rank_t.transpose(0, 2, 1), x1, mod, cap)


def kernel(x, c, positions, norm1_g, norm2_g, w_ada, b_ada, w_in, mlstm_conv_w, mlstm_gate_b, mlstm_norm_g,
           diff_qk_g, diff_lambda, diff_subln_g, w_out, w_router, w_gate_e, w_up_e, w_down_e):
    for l in range(norm1_g.shape[0]):
        x = _layer(l, x, c, positions, norm1_g[l], norm2_g[l], w_ada[l], b_ada[l], w_in[l], mlstm_conv_w[l],
                   mlstm_gate_b[l], mlstm_norm_g[l], diff_qk_g[l], diff_lambda[l], diff_subln_g[l], w_out[l],
                   w_router[l], w_gate_e[l], w_up_e[l], w_down_e[l])
    return x
```

```python
import functools
import math

import numpy as np
import jax
import jax.numpy as jnp
from jax import lax
from jax.experimental import pallas as pl
from jax.experimental.pallas import tpu as pltpu

F32 = jnp.float32
BF16 = jnp.bfloat16
I32 = jnp.int32

M_HEADS = 4
A_HEADS = 4
CONV_K = 5
ROPE_THETA = 500000.0
EPS = 1e-6
CAP_FACTOR = 2

MLSTM_CHUNK = 256
SCAN_UNROLL = 4
ROW_TILE = 512
OUTPROJ_TILE = 1024
OUTPROJ_SUB = 256
ATT_TQ = 512
ATT_TK = 512
ATT_GROUPS = 4
TOK_TILE = 256
BF16_ROWS = 16
AFF_LANES = 128
GATHER_UNROLL = 2
GATHER_EXPERTS = 4
COMBINE_TILES = 2
SMALL_WIN = 64
FFN_ROWS = 2048
FFN_HIDDEN = 512
FFN_SUB = 512
VMEM_LIMIT = 56 * 1024 * 1024


def _cparams(sem):
    return pltpu.CompilerParams(dimension_semantics=sem, vmem_limit_bytes=VMEM_LIMIT)


def _sigmoid(v):
    return 1.0 / (1.0 + jnp.exp(-v))


def _rms(v):
    return v * lax.rsqrt(jnp.mean(v * v, axis=-1, keepdims=True) + EPS)


def _ada_body(c_ref, w_ref, b_ref, o_ref):
    c = c_ref[...]
    s = (c * _sigmoid(c)).astype(BF16)
    o_ref[...] = jnp.dot(s, w_ref[...].astype(BF16), preferred_element_type=F32) + b_ref[...]


def _ada(c, w_ada, b_ada):
    B, D = c.shape
    N = w_ada.shape[1]
    tn = D
    return pl.pallas_call(
        _ada_body,
        grid=(N // tn,),
        in_specs=[pl.BlockSpec((B, D), lambda j: (0, 0)),
                  pl.BlockSpec((D, tn), lambda j: (0, j)),
                  pl.BlockSpec((1, tn), lambda j: (0, j))],
        out_specs=pl.BlockSpec((B, tn), lambda j: (0, j)),
        out_shape=jax.ShapeDtypeStruct((B, N), F32),
        compiler_params=_cparams(("arbitrary",)),
        name="ada",
    )(c, w_ada, b_ada.reshape(1, N))


def _inproj_body(x_ref, mod_ref, g_ref, wm_ref, wa_ref, wg_ref, bg_ref, pos_ref, invf_ref, fl_ref, s1_ref, s2_ref,
                 gm_ref, gq_ref, gk_ref, pm_ref, qn_ref, kn_ref, v_ref, gt_ref, *, W, q_scale):
    s1 = s1_ref[...]
    s2 = s2_ref[...]
    gm = gm_ref[...]
    lanes = gm.shape[0]
    tm = x_ref.shape[1]

    ang_t = invf_ref[...] * pos_ref[0, 0].astype(F32)
    cs_c = jnp.cos(ang_t).T
    sn_c = jnp.sin(ang_t).T
    fl = fl_ref[...]
    cs = jnp.ones((tm, lanes), F32)
    sn = jnp.zeros((tm, lanes), F32)
    for f in range(invf_ref.shape[0]):
        cs = jnp.where(fl == f, cs_c[:, f:f + 1], cs)
        sn = jnp.where(fl == f, sn_c[:, f:f + 1], sn)

    x = x_ref[0]
    h = _rms(x) * g_ref[...]
    h = (h * (1.0 + mod_ref[0, 1:2, :]) + mod_ref[0, 0:1, :]).astype(BF16)
    pa = jnp.dot(h, wa_ref[...], preferred_element_type=F32)
    v_ref[0] = pa[:, 2 * W:3 * W].astype(BF16)

    def prep(col0, j, gain_ref, dst_ref, scale):
        t = pa[:, col0 + j * lanes:col0 + (j + 1) * lanes]
        ms = jnp.dot((t * t).astype(BF16), gm, preferred_element_type=F32)
        y = t * lax.rsqrt(ms + EPS) * gain_ref[...]
        half = pltpu.roll(y, lanes - 8, 1) * s1 + pltpu.roll(y, 8, 1) * s2
        r = y * cs + half * sn
        dst_ref[0, :, j * lanes:(j + 1) * lanes] = (r * scale).astype(BF16)

    pieces = [(0, j, gq_ref, qn_ref, q_scale) for j in range(W // lanes)] \
        + [(W, j, gk_ref, kn_ref, 1.0) for j in range(W // lanes)]
    cw = pm_ref.shape[2] // len(pieces)
    for i, piece in enumerate(pieces):
        pm_ref[0, :, i * cw:(i + 1) * cw] = jnp.dot(h, wm_ref[:, i * cw:(i + 1) * cw],
                                                     preferred_element_type=F32).astype(BF16)
        prep(*piece)
    gt_ref[0] = jnp.dot(h, wg_ref[...], preferred_element_type=F32) + bg_ref[...]


def _inproj(x, mod, g1, wm, wa, wg, bg, positions, qk_g):
    B, S, D = x.shape
    tm = min(ROW_TILE, S)
    nm, na, ng = wm.shape[1], wa.shape[1], wg.shape[1]
    W = na // 3
    hd = qk_g.shape[1]
    rot = hd // 4
    half = rot // 2
    lanes = 128
    d = np.arange(lanes) % hd
    inv_freq = (ROPE_THETA ** (-np.arange(0, rot, 2, dtype=np.float32) / rot)).astype(np.float32)
    flane = np.where(d < rot, d % half, -1).astype(np.int32)[None, :]
    s1 = np.where(d < half, -1.0, 0.0).astype(np.float32)[None, :]
    s2 = np.where((d >= half) & (d < rot), 1.0, 0.0).astype(np.float32)[None, :]
    gm = (np.arange(lanes)[:, None] // hd == np.arange(lanes)[None, :] // hd).astype(np.float32) / hd
    gq = jnp.tile(qk_g[0], lanes // hd)[None, :]
    gk = jnp.tile(qk_g[1], lanes // hd)[None, :]
    q_scale = (hd ** -0.5) * math.log2(math.e)
    body = functools.partial(_inproj_body, W=W, q_scale=q_scale)
    const = lambda shape: pl.BlockSpec(shape, lambda b, i: (0, 0))
    rows = lambda n: pl.BlockSpec((1, tm, n), lambda b, i: (b, i, 0))
    return pl.pallas_call(
        body,
        grid=(B, S // tm),
        in_specs=[rows(D),
                  pl.BlockSpec((1, 6, D), lambda b, i: (b, 0, 0)),
                  const((1, D)), const((D, nm)), const((D, na)), const((D, ng)), const((1, ng)),
                  pl.BlockSpec((1, 1, 1, tm), lambda b, i: (b, i, 0, 0)),
                  const((half, 1)), const((1, lanes)),
                  const((1, lanes)), const((1, lanes)), const((lanes, lanes)),
                  const((1, lanes)), const((1, lanes))],
        out_specs=[rows(nm), rows(W), rows(W), rows(W), rows(ng)],
        out_shape=[jax.ShapeDtypeStruct((B, S, nm), BF16),
                   jax.ShapeDtypeStruct((B, S, W), BF16),
                   jax.ShapeDtypeStruct((B, S, W), BF16),
                   jax.ShapeDtypeStruct((B, S, W), BF16),
                   jax.ShapeDtypeStruct((B, S, ng), F32)],
        compiler_params=_cparams(("arbitrary", "arbitrary")),
        name="inproj",
    )(x, mod, g1, wm, wa, wg, bg, positions.reshape(B, S // tm, 1, tm), jnp.asarray(inv_freq[:, None]),
      jnp.asarray(flane), jnp.asarray(s1), jnp.asarray(s2), jnp.asarray(gm, dtype=BF16), gq, gk)


def _log_sigmoid(v):
    return jnp.minimum(v, 0.0) - jnp.log(1.0 + jnp.exp(-jnp.abs(v)))


def _mlstm_body(q_ref, k_ref, v_ref, o_ref, cwq_ref, cwk_ref, gt_ref, ng_ref, y_ref,
                xp_s, q_s, kt_s, va_s, hf_s, a_s, *, S, L, dh):
    hd = pl.program_id(1)
    nc = S // L
    win = L + 16

    def conv(src_ref, cw_ref, post):
        xp_s[0:8, :] = jnp.zeros((8, dh), F32)
        xp_s[S + 8:S + 16, :] = jnp.zeros((8, dh), F32)

        def fill(c, _):
            base = pl.multiple_of(c * L, L)
            xp_s[pl.ds(pl.multiple_of(base + 8, 8), L), :] = src_ref[0, pl.ds(base, L), :].astype(F32)
            return 0
        lax.fori_loop(0, nc, fill, 0)

        def body(c, _):
            base = pl.multiple_of(c * L, L)
            w = xp_s[pl.ds(base, win), :]
            acc = None
            for j in range(CONV_K):
                sh = (CONV_K // 2 - j) % win
                r = w if sh == 0 else pltpu.roll(w, sh, 0)
                term = r[8:8 + L, :] * cw_ref[j:j + 1, :]
                acc = term if acc is None else acc + term
            post(base, acc * _sigmoid(acc))
            return 0
        lax.fori_loop(0, nc, body, 0)

    def post_q(base, a):
        q_s[pl.ds(base, L), :] = a.astype(BF16)

    def post_k(base, a):
        kt_s[:, pl.ds(base, L)] = (a * (dh ** -0.5)).T.astype(BF16)

    conv(q_ref, cwq_ref, post_q)
    conv(k_ref, cwk_ref, post_k)

    ones_col = (lax.broadcasted_iota(I32, (L, dh), 1) == 0).astype(BF16)

    def fill_v(c, _):
        base = pl.multiple_of(c * L, L)
        va_s[pl.ds(base, L), 0:dh] = v_ref[0, pl.ds(base, L), :]
        va_s[pl.ds(base, L), dh:2 * dh] = ones_col
        return 0
    lax.fori_loop(0, nc, fill_v, 0)

    ri = lax.broadcasted_iota(I32, (L, L), 0)
    ci = lax.broadcasted_iota(I32, (L, L), 1)

    def scan(reverse, gi, gf, emit):
        logf_all = _log_sigmoid(gt_ref[0, gf])
        tri = ((ri >= ci) if reverse else (ri <= ci)).astype(F32)
        a_s[...] = jnp.dot(logf_all, tri, preferred_element_type=F32, precision=lax.Precision.HIGHEST)
        valid = (ci >= ri) if reverse else (ci <= ri)

        def body(step, carry):
            caug, m = carry
            c = (nc - 1 - step) if reverse else step
            base = pl.multiple_of(c * L, L)
            i_row = gt_ref[0, gi, pl.ds(c, 1), :]
            logf = _log_sigmoid(gt_ref[0, gf, pl.ds(c, 1), :])
            a_row = a_s[pl.ds(c, 1), :]
            a_col = jnp.sum(jnp.where(valid, logf, 0.0), axis=1, keepdims=True)
            g = jnp.sum(logf, axis=1, keepdims=True)
            dmat = jnp.where(valid, a_col - a_row + i_row, -jnp.inf)
            dmax = jnp.max(dmat, axis=1, keepdims=True)
            q_c = q_s[pl.ds(base, L), :]
            kt_c = kt_s[:, pl.ds(base, L)]
            va_c = va_s[pl.ds(base, L), :]
            qk = jnp.dot(q_c, kt_c, preferred_element_type=F32)
            p = (qk * jnp.exp(dmat - dmax)).astype(BF16)
            intra = jnp.dot(p, va_c, preferred_element_type=F32)
            inter = a_col + m
            mj = jnp.maximum(inter, dmax)
            nd = (jnp.exp(dmax - mj) * intra
                  + jnp.exp(inter - mj) * jnp.dot(q_c, caug.astype(BF16), preferred_element_type=F32))
            den = jnp.maximum(jnp.abs(nd[:, dh:dh + 1]), jnp.exp(-mj))
            emit(base, nd[:, 0:dh] * (1.0 / den))
            wlog = g - a_row + i_row
            mloc = jnp.max(wlog, axis=1, keepdims=True)
            kw = (kt_c.astype(F32) * jnp.exp(wlog - mloc)).astype(BF16)
            cloc = jnp.dot(kw, va_c, preferred_element_type=F32)
            m_new = jnp.maximum(g + m, mloc)
            caug = jnp.exp(g + m - m_new) * caug + jnp.exp(mloc - m_new) * cloc
            return caug, m_new

        lax.fori_loop(0, nc, body, (jnp.zeros((dh, 2 * dh), F32), jnp.zeros((1, 1), F32)),
                      unroll=min(SCAN_UNROLL, nc))

    def emit_f(base, h):
        hf_s[pl.ds(base, L), :] = h

    def emit_b(base, h):
        hn = _rms(hf_s[pl.ds(base, L), :] + h) * ng_ref[...]
        og = o_ref[0, pl.ds(base, L), :].astype(F32)
        y_ref[0, pl.ds(base, L), :] = (_sigmoid(og) * hn).astype(BF16)

    scan(False, hd, M_HEADS + hd, emit_f)
    scan(True, 2 * M_HEADS + hd, 3 * M_HEADS + hd, emit_b)


def _mlstm(pm, conv_w, gt, norm_g):
    B, S, _ = pm.shape
    dh = norm_g.shape[1] // M_HEADS
    L = min(MLSTM_CHUNK, S)
    nc = S // L
    H = M_HEADS
    body = functools.partial(_mlstm_body, S=S, L=L, dh=dh)
    return pl.pallas_call(
        body,
        grid=(B, H),
        in_specs=[pl.BlockSpec((1, S, dh), lambda b, h: (b, 0, h)),
                  pl.BlockSpec((1, S, dh), lambda b, h: (b, 0, H + h)),
                  pl.BlockSpec((1, S, dh), lambda b, h: (b, 0, 2 * H + h)),
                  pl.BlockSpec((1, S, dh), lambda b, h: (b, 0, 3 * H + h)),
                  pl.BlockSpec((CONV_K, dh), lambda b, h: (0, h)),
                  pl.BlockSpec((CONV_K, dh), lambda b, h: (0, H + h)),
                  pl.BlockSpec((1, 4 * H, nc, L), lambda b, h: (b, 0, 0, 0)),
                  pl.BlockSpec((1, dh), lambda b, h: (0, h))],
        out_specs=pl.BlockSpec((1, S, dh), lambda b, h: (b, 0, h)),
        out_shape=jax.ShapeDtypeStruct((B, S, H * dh), BF16),
        scratch_shapes=[pltpu.VMEM((S + 16, dh), F32),
                        pltpu.VMEM((S, dh), BF16),
                        pltpu.VMEM((dh, S), BF16),
                        pltpu.VMEM((S, 2 * dh), BF16),
                        pltpu.VMEM((S, dh), F32),
                        pltpu.VMEM((nc, L), F32)],
        compiler_params=_cparams(("arbitrary", "arbitrary")),
        name="mlstm",
    )(pm, pm, pm, pm, conv_w, conv_w, gt, norm_g)


def _attn_body(q_ref, k_ref, v_ref, lam_ref, sg_ref, y_ref, va_s, kt_s, s_s, *, S, tq, tk, dv, lam_init):
    nk = S // tk
    lanes = 128

    @pl.when(pl.program_id(2) == 0)
    def _():
        ones_col = (lax.broadcasted_iota(I32, (tk, dv), 1) == 0).astype(BF16)
        for kb in range(nk):
            va_s[kb * tk:(kb + 1) * tk, 0:dv] = v_ref[0, kb * tk:(kb + 1) * tk, :]
            va_s[kb * tk:(kb + 1) * tk, dv:2 * dv] = ones_col
            kt_s[:, kb * tk:(kb + 1) * tk] = k_ref[0, kb * tk:(kb + 1) * tk, :].astype(F32).T.astype(BF16)

    lp = lam_ref[...]
    lam = (jnp.exp(jnp.sum(lp[0:1, :] * lp[1:2, :], axis=1, keepdims=True))
           - jnp.exp(jnp.sum(lp[2:3, :] * lp[3:4, :], axis=1, keepdims=True)) + lam_init)

    q = q_ref[0]
    lane = lax.broadcasted_iota(I32, q.shape, 1)
    hd = q.shape[1] // 2

    qs = jnp.concatenate([jnp.where(lane < hd, q, jnp.zeros_like(q)),
                          jnp.where(lane >= hd, q, jnp.zeros_like(q))], axis=0)
    G = ATT_GROUPS
    rg = 2 * tq // G
    mparts, mbs, accs = [None] * G, [None] * G, [None] * G

    def score(g, kb):
        rows = slice(g * rg, (g + 1) * rg)
        s = jnp.dot(qs[rows, :], kt_s[:, kb * tk:(kb + 1) * tk], preferred_element_type=F32)
        s_s[rows, kb * tk:(kb + 1) * tk] = s
        for j in range(tk // lanes):
            blk = s[:, j * lanes:(j + 1) * lanes]
            mparts[g] = blk if mparts[g] is None else jnp.maximum(mparts[g], blk)

    def rowmax(g):
        mbs[g] = jnp.broadcast_to(jnp.max(mparts[g], axis=1, keepdims=True), (rg, lanes))

    def weigh(g, kb):
        rows = slice(g * rg, (g + 1) * rg)
        p = jnp.concatenate(
            [jnp.exp2(s_s[rows, kb * tk + j * lanes:kb * tk + (j + 1) * lanes] - mbs[g]).astype(BF16)
             for j in range(tk // lanes)], axis=1)
        part = jnp.dot(p, va_s[kb * tk:(kb + 1) * tk, :], preferred_element_type=F32)
        accs[g] = part if accs[g] is None else accs[g] + part

    first, second = range(G // 2), range(G // 2, G)
    for kb in range(nk):
        for g in first:
            score(g, kb)
    for g in first:
        rowmax(g)
    for kb in range(nk):
        for g in second:
            score(g, kb)
        for g in first:
            weigh(g, kb)
    for g in second:
        rowmax(g)
    for kb in range(nk):
        for g in second:
            weigh(g, kb)
    acc = jnp.concatenate(accs, axis=0)
    r = acc[:, 0:dv] * (1.0 / acc[:, dv:dv + 1])
    o = r[0:tq, :] - lam * r[tq:2 * tq, :]
    y_ref[0] = (_rms(o) * sg_ref[...] * (1.0 - lam_init)).astype(BF16)


def _attn(qn, kn, va, lam_p, subln_g, layer_idx):
    B, S, W = qn.shape
    dv = W // A_HEADS
    tq = min(ATT_TQ, S)
    tk = min(ATT_TK, S)
    lam_init = 0.8 - 0.6 * math.exp(-0.3 * layer_idx)
    body = functools.partial(_attn_body, S=S, tq=tq, tk=tk, dv=dv, lam_init=lam_init)
    return pl.pallas_call(
        body,
        grid=(B, A_HEADS, S // tq),
        in_specs=[pl.BlockSpec((1, tq, dv), lambda b, h, i: (b, i, h)),
                  pl.BlockSpec((1, S, dv), lambda b, h, i: (b, 0, h)),
                  pl.BlockSpec((1, S, dv), lambda b, h, i: (b, 0, h)),
                  pl.BlockSpec(lam_p.shape, lambda b, h, i: (0, 0)),
                  pl.BlockSpec((1, dv), lambda b, h, i: (0, 0))],
        out_specs=pl.BlockSpec((1, tq, dv), lambda b, h, i: (b, i, h)),
        out_shape=jax.ShapeDtypeStruct((B, S, W), BF16),
        scratch_shapes=[pltpu.VMEM((S, 2 * dv), BF16), pltpu.VMEM((dv, S), BF16), pltpu.VMEM((2 * tq, S), F32)],
        compiler_params=_cparams(("arbitrary", "arbitrary", "arbitrary")),
        name="attn",
    )(qn, kn, va, lam_p, subln_g.reshape(1, dv))


def _outproj_body(ym_ref, ya_ref, x_ref, mod_ref, wom_ref, woa_ref, g2_ref, wrh_ref, wrl_ref,
                  x1_ref, h2_ref, aff_ref, *, sub):
    for r in range(x_ref.shape[1] // sub):
        rows = slice(r * sub, (r + 1) * sub)
        mix = (jnp.dot(ym_ref[0, rows, :], wom_ref[...], preferred_element_type=F32)
               + jnp.dot(ya_ref[0, rows, :], woa_ref[...], preferred_element_type=F32))
        x1 = x_ref[0, rows, :] + mod_ref[0, 2:3, :] * mix
        x1_ref[0, rows, :] = x1
        h2 = _rms(x1) * g2_ref[...] * (1.0 + mod_ref[0, 4:5, :]) + mod_ref[0, 3:4, :]
        hh = h2.astype(BF16)
        h2_ref[0, rows, :] = hh
        hl = (h2 - hh.astype(F32)).astype(BF16)
        logits = (jnp.dot(hh, wrh_ref[...], preferred_element_type=F32)
                  + jnp.dot(hl, wrh_ref[...], preferred_element_type=F32)
                  + jnp.dot(hh, wrl_ref[...], preferred_element_type=F32))
        ex = jnp.exp(logits - jnp.max(logits, axis=1, keepdims=True))
        aff_ref[0, rows, :] = ex * (1.0 / jnp.sum(ex, axis=1, keepdims=True))


def _outproj(ym, ya, x, mod, wom, woa, g2, wrh, wrl):
    B, S, D = x.shape
    tm = min(OUTPROJ_TILE, S)
    wm, wa, E = ym.shape[2], ya.shape[2], wrh.shape[1]
    full = lambda shape: pl.BlockSpec(shape, lambda b, i: (0, 0))
    return pl.pallas_call(
        functools.partial(_outproj_body, sub=min(OUTPROJ_SUB, tm)),
        grid=(B, S // tm),
        in_specs=[pl.BlockSpec((1, tm, wm), lambda b, i: (b, i, 0)),
                  pl.BlockSpec((1, tm, wa), lambda b, i: (b, i, 0)),
                  pl.BlockSpec((1, tm, D), lambda b, i: (b, i, 0)),
                  pl.BlockSpec((1, 6, D), lambda b, i: (b, 0, 0)),
                  full((wm, D)), full((wa, D)), full((1, D)), full((D, E)), full((D, E))],
        out_specs=[pl.BlockSpec((1, tm, D), lambda b, i: (b, i, 0)),
                   pl.BlockSpec((1, tm, D), lambda b, i: (b, i, 0)),
                   pl.BlockSpec((1, tm, E), lambda b, i: (b, i, 0))],
        out_shape=[jax.ShapeDtypeStruct((B, S, D), F32),
                   jax.ShapeDtypeStruct((B, S, D), BF16),
                   jax.ShapeDtypeStruct((B, S, E), F32)],
        compiler_params=_cparams(("arbitrary", "arbitrary")),
        name="outproj",
    )(ym, ya, x, mod, wom, woa, g2, wrh, wrl)


def _topk_body(a_ref, rank_ref, starts_ref, *, S, cap, tb):
    bits = lax.bitcast_convert_type(a_ref[0], I32)
    E = bits.shape[0]

    def search(i, thr):
        cand = thr | jnp.left_shift(jnp.int32(1), 30 - i)
        cnt = jnp.sum((bits >= cand).astype(I32), axis=1, keepdims=True)
        return jnp.where(cnt >= cap, cand, thr)
    thr = lax.fori_loop(0, 31, search, jnp.zeros((E, 1), I32))

    gt = bits > thr
    eq = bits == thr
    need = cap - jnp.sum(gt.astype(I32), axis=1, keepdims=True)

    nt = S // tb
    upper = (lax.broadcasted_iota(I32, (tb, tb), 0) < lax.broadcasted_iota(I32, (tb, tb), 1)).astype(BF16)

    def excl_cumsum(mask, with_starts):
        off = jnp.zeros((E, 1), F32)
        outs, starts = [], []
        for t in range(nt):
            blk = mask[:, t * tb:(t + 1) * tb].astype(F32).astype(BF16)
            outs.append(jnp.dot(blk, upper, preferred_element_type=F32) + off)
            starts.append(off)
            off = off + jnp.sum(blk.astype(F32), axis=1, keepdims=True)
        res = jnp.concatenate(outs, axis=1).astype(I32)
        return (res, jnp.concatenate(starts, axis=1).astype(I32)) if with_starts else res

    sel = gt | (eq & (excl_cumsum(eq, False) < need))
    rank, starts = excl_cumsum(sel, True)
    rank_ref[0] = jnp.where(sel, rank, -1)
    starts_ref[0] = starts


def _topk(aff_t, cap):
    B, E, S = aff_t.shape
    tb = min(TOK_TILE, S)
    nt = S // tb
    body = functools.partial(_topk_body, S=S, cap=cap, tb=tb)
    return pl.pallas_call(
        body,
        grid=(B,),
        in_specs=[pl.BlockSpec((1, E, S), lambda b: (b, 0, 0))],
        out_specs=[pl.BlockSpec((1, E, S), lambda b: (b, 0, 0)),
                   pl.BlockSpec((1, E, nt), lambda b: (b, 0, 0))],
        out_shape=[jax.ShapeDtypeStruct((B, E, S), I32), jax.ShapeDtypeStruct((B, E, nt), I32)],
        compiler_params=_cparams(("arbitrary",)),
        name="topk",
    )(aff_t)


def _gather_body(starts_ref, h2_ref, aff_ref, rank_ref, xe_ref, acc_s, *, S, E, EG, cap, tb):
    b = pl.program_id(0)
    g = pl.program_id(1)
    nt = S // tb
    D = h2_ref.shape[2]
    DX = acc_s.shape[2]
    big = tb + BF16_ROWS
    sm = min(SMALL_WIN, big)
    head = lax.broadcasted_iota(I32, (BF16_ROWS, DX), 0)

    def bounds(j, t):
        row = (b * E + g * EG + j) * nt
        r0 = starts_ref[row + t]
        r_end = jnp.where(t == nt - 1, cap, starts_ref[row + jnp.minimum(t + 1, nt - 1)])
        r0a = pl.multiple_of((r0 // BF16_ROWS) * BF16_ROWS, BF16_ROWS)
        return r0, r_end, r0a

    def put(j, r0, r0a, rows, hit, off, w):
        a = jnp.sum(jnp.where(hit, aff_ref[0, j, :, pl.ds(off, tb)], 0.0), axis=1, keepdims=True)
        a1 = a.astype(BF16).astype(F32)
        a2 = (a - a1).astype(BF16).astype(F32)
        a3 = a - a1 - a2
        xlane = lax.broadcasted_iota(I32, (w, AFF_LANES), 1)
        ext = jnp.where(xlane == 0, a1, jnp.where(xlane == 1, a2, jnp.where(xlane == 2, a3, 0.0))).astype(BF16)
        full = jnp.concatenate([rows, ext], axis=1)
        keep = acc_s[j, pl.ds(r0a, BF16_ROWS), :]
        acc_s[j, pl.ds(r0a, BF16_ROWS), :] = jnp.where(head < r0 - r0a, keep, full[0:BF16_ROWS, :])
        acc_s[j, pl.ds(r0a + BF16_ROWS, w - BF16_ROWS), :] = full[BF16_ROWS:, :]

    def tile(t, _):
        off = pl.multiple_of(t * tb, tb)
        bnds = [bounds(j, t) for j in range(EG)]
        fits = None
        for r0, r_end, r0a in bnds:
            f = r_end - r0a < sm
            fits = f if fits is None else (fits & f)

        @pl.when(fits)
        def _():
            sub = lax.broadcasted_iota(I32, (sm, tb), 0)
            hits = [sub == rank_ref[0, j, :, pl.ds(off, tb)] - bnds[j][2] for j in range(EG)]
            onehot = jnp.concatenate([h.astype(F32) for h in hits], axis=0).astype(BF16)
            rows = jnp.dot(onehot, h2_ref[0, pl.ds(off, tb), :], preferred_element_type=F32).astype(BF16)
            for j in range(EG):
                put(j, bnds[j][0], bnds[j][2], rows[j * sm:(j + 1) * sm, :], hits[j], off, sm)

        @pl.when(jnp.logical_not(fits))
        def _():
            sub = lax.broadcasted_iota(I32, (big, tb), 0)
            for j in range(EG):
                hit = sub == rank_ref[0, j, :, pl.ds(off, tb)] - bnds[j][2]
                rows = jnp.dot(hit.astype(F32).astype(BF16), h2_ref[0, pl.ds(off, tb), :],
                               preferred_element_type=F32).astype(BF16)
                put(j, bnds[j][0], bnds[j][2], rows, hit, off, big)
        return 0

    for j in range(EG):
        acc_s[j, 0:BF16_ROWS, :] = jnp.zeros((BF16_ROWS, DX), BF16)
    lax.fori_loop(0, nt, tile, 0, unroll=min(GATHER_UNROLL, nt))
    for j in range(EG):
        xe_ref[j] = acc_s[j, 0:cap, :]


def _gather(starts, h2, aff_t, rank_t, cap):
    B, S, D = h2.shape
    E = rank_t.shape[1]
    EG = min(GATHER_EXPERTS, E)
    tb = min(TOK_TILE, S)
    DX = D + AFF_LANES
    body = functools.partial(_gather_body, S=S, E=E, EG=EG, cap=cap, tb=tb)
    grid_spec = pltpu.PrefetchScalarGridSpec(
        num_scalar_prefetch=1,
        grid=(B, E // EG),
        in_specs=[pl.BlockSpec((1, S, D), lambda b, g, st: (b, 0, 0)),
                  pl.BlockSpec((1, EG, 1, S), lambda b, g, st: (b, g, 0, 0)),
                  pl.BlockSpec((1, EG, 1, S), lambda b, g, st: (b, g, 0, 0))],
        out_specs=pl.BlockSpec((EG, cap, DX), lambda b, g, st: (g, b, 0)),
        scratch_shapes=[pltpu.VMEM((EG, cap + tb + BF16_ROWS, DX), BF16)],
    )
    return pl.pallas_call(
        body,
        grid_spec=grid_spec,
        out_shape=jax.ShapeDtypeStruct((E, B * cap, DX), BF16),
        compiler_params=_cparams(("arbitrary", "arbitrary")),
        name="gather",
    )(starts.reshape(-1), h2, aff_t.reshape(B, E, 1, S), rank_t.reshape(B, E, 1, S))


def _ffn_body(x_ref, wg_ref, wu_ref, wd_ref, y_ref, acc_s, *, tt, sub, nfc):
    fc = pl.program_id(2)
    D = y_ref.shape[2]

    def mm(a, w):
        return lax.dot_general(a, w, (((1,), (0,)), ((), ())), preferred_element_type=F32)

    if nfc > 1:
        @pl.when((pl.program_id(0) == 0) & (pl.program_id(1) == 0) & (fc == 0))
        def _():
            acc_s[...] = jnp.zeros(acc_s.shape, F32)

    for i in range(tt // sub):
        rows = slice(i * sub, (i + 1) * sub)
        x = x_ref[0, rows, 0:D]
        g = mm(x, wg_ref[0])
        u = mm(x, wu_ref[0])
        a = (g * _sigmoid(g) * u).astype(BF16)
        total = mm(a, wd_ref[0])
        if nfc > 1:
            total = jnp.where(fc == 0, total, acc_s[rows, :] + total)
            acc_s[rows, :] = total
        val = jnp.sum(x_ref[0, rows, D:].astype(F32), axis=1, keepdims=True)
        y_ref[0, rows, :] = (total * val).astype(BF16)


def _ffn(xe, w_gate, w_up, w_down):
    E, T, DX = xe.shape
    D = w_gate.shape[1]
    F = w_gate.shape[2]
    tt = min(FFN_ROWS, T)
    tf = min(FFN_HIDDEN, F)
    sub = min(FFN_SUB, tt)
    body = functools.partial(_ffn_body, tt=tt, sub=sub, nfc=F // tf)
    return pl.pallas_call(
        body,
        grid=(E, T // tt, F // tf),
        in_specs=[pl.BlockSpec((1, tt, DX), lambda e, t, f: (e, t, 0)),
                  pl.BlockSpec((1, D, tf), lambda e, t, f: (e, 0, f)),
                  pl.BlockSpec((1, D, tf), lambda e, t, f: (e, 0, f)),
                  pl.BlockSpec((1, tf, D), lambda e, t, f: (e, f, 0))],
        out_specs=pl.BlockSpec((1, tt, D), lambda e, t, f: (e, t, 0)),
        out_shape=jax.ShapeDtypeStruct((E, T, D), BF16),
        scratch_shapes=[pltpu.VMEM((tt, D), F32)],
        compiler_params=_cparams(("arbitrary", "arbitrary", "arbitrary")),
        name="ffn",
    )(xe, w_gate, w_up, w_down)


def _combine_body(starts_ref, y_ref, rank_ref, x1_ref, mod_ref, o_ref, ycat_s, *, S, E, cap, tb):
    b = pl.program_id(0)
    nt = S // tb
    nsub = o_ref.shape[1] // tb
    kw = min(tb, cap)
    sm = min(SMALL_WIN, kw)
    lanes = 128
    per = lanes // sm
    gate = mod_ref[0, 5:6, :]

    def window(u, e, w):
        t = pl.program_id(1) * nsub + u
        row = (b * E + e) * nt
        r0 = starts_ref[row + t]
        r_end = jnp.where(t == nt - 1, cap, starts_ref[row + jnp.minimum(t + 1, nt - 1)])
        r0a = pl.multiple_of(jnp.minimum((r0 // BF16_ROWS) * BF16_ROWS, cap - w), BF16_ROWS)
        return r0a, r_end > r0a + w

    fits = None
    for u in range(nsub):
        for e in range(E):
            f = jnp.logical_not(window(u, e, sm)[1])
            fits = f if fits is None else (fits & f)

    @pl.when(fits)
    def _():
        lane = lax.broadcasted_iota(I32, (tb, lanes), 1)
        for u in range(nsub):
            rows = slice(u * tb, (u + 1) * tb)
            rank = rank_ref[0, rows, :]
            tiles = []
            for k in range(E // per):
                hit = None
                for i in range(per):
                    e = k * per + i
                    r0a, _ = window(u, e, sm)
                    ycat_s[u, e * sm:(e + 1) * sm, :] = y_ref[e, pl.ds(r0a, sm), :]
                    col = rank[:, e:e + 1] - r0a
                    tgt = jnp.where((col >= 0) & (col < sm), col + i * sm, -1)
                    h = lane == tgt
                    hit = h if hit is None else (hit | h)
                tiles.append(hit.astype(F32).astype(BF16))
            onehot = jnp.concatenate(tiles, axis=1)
            o_ref[0, rows, :] = x1_ref[0, rows, :] + gate * jnp.dot(onehot, ycat_s[u],
                                                                   preferred_element_type=F32)

    @pl.when(jnp.logical_not(fits))
    def _():
        lane = lax.broadcasted_iota(I32, (tb, kw), 1)
        for u in range(nsub):
            rows = slice(u * tb, (u + 1) * tb)
            rank = rank_ref[0, rows, :]
            acc = None
            for e in range(E):
                r0a, _ = window(u, e, kw)
                onehot = (rank[:, e:e + 1] - r0a == lane).astype(F32).astype(BF16)
                part = jnp.dot(onehot, y_ref[e, pl.ds(r0a, kw), :], preferred_element_type=F32)
                acc = part if acc is None else acc + part
            o_ref[0, rows, :] = x1_ref[0, rows, :] + gate * acc

            if cap > kw:
                for e in range(E):
                    r0a, over = window(u, e, kw)

                    @pl.when(over)
                    def _(e=e, r0a=r0a, rows=rows):
                        r1a = pl.multiple_of(jnp.minimum(r0a + kw, cap - kw), BF16_ROWS)
                        rk = rank_ref[0, rows, e:e + 1]
                        oh = ((rk - r1a == lane) & (rk >= r0a + kw)).astype(F32).astype(BF16)
                        o_ref[0, rows, :] += gate * jnp.dot(oh, y_ref[e, pl.ds(r1a, kw), :],
                                                            preferred_element_type=F32)


def _combine(starts, y, rank, x1, mod, cap):
    B, S, D = x1.shape
    E = rank.shape[2]
    tb = min(TOK_TILE, S)
    nsub = min(COMBINE_TILES, S // tb)
    rows = nsub * tb
    body = functools.partial(_combine_body, S=S, E=E, cap=cap, tb=tb)
    grid_spec = pltpu.PrefetchScalarGridSpec(
        num_scalar_prefetch=1,
        grid=(B, S // rows),
        in_specs=[pl.BlockSpec((E, cap, D), lambda b, t, st: (0, b, 0)),
                  pl.BlockSpec((1, rows, E), lambda b, t, st: (b, t, 0)),
                  pl.BlockSpec((1, rows, D), lambda b, t, st: (b, t, 0)),
                  pl.BlockSpec((1, 6, D), lambda b, t, st: (b, 0, 0))],
        out_specs=pl.BlockSpec((1, rows, D), lambda b, t, st: (b, t, 0)),
        scratch_shapes=[pltpu.VMEM((nsub, E * min(SMALL_WIN, tb, cap), D), BF16)],
    )
    return pl.pallas_call(
        body,
        grid_spec=grid_spec,
        out_shape=jax.ShapeDtypeStruct((B, S, D), F32),
        compiler_params=_cparams(("arbitrary", "arbitrary")),
        name="combine",
    )(starts.reshape(-1), y, rank, x1, mod)


def _layer(layer_idx, x, c, positions, norm1_g, norm2_g, w_ada, b_ada, w_in, conv_w, gate_b, mnorm_g, qk_g,
           lam_p, subln_g, w_out, w_router, w_gate_e, w_up_e, w_down_e):
    B, S, D = x.shape
    E = w_router.shape[1]
    d_m = mnorm_g.shape[0]
    n_g = 4 * M_HEADS
    d_a = w_out.shape[0] - d_m
    cap = CAP_FACTOR * S // E
    L = min(MLSTM_CHUNK, S)

    mod = _ada(c, w_ada, b_ada).reshape(B, 6, D)

    wm = w_in[:, :4 * d_m].astype(BF16)
    wg = w_in[:, 4 * d_m:4 * d_m + n_g].astype(BF16)
    wa = w_in[:, 4 * d_m + n_g:].astype(BF16)
    pm, qn, kn, va, gates = _inproj(x, mod, norm1_g.reshape(1, D), wm, wa, wg, gate_b.reshape(1, n_g),
                                    positions, qk_g)

    gt = gates.reshape(B, S // L, L, n_g).transpose(0, 3, 1, 2)
    ym = _mlstm(pm, conv_w, gt, mnorm_g.reshape(1, d_m))
    ya = _attn(qn, kn, va, lam_p, subln_g, layer_idx)

    wo = w_out.astype(BF16)
    wr_hi = w_router.astype(BF16)
    wr_lo = (w_router - wr_hi.astype(F32)).astype(BF16)
    x1, h2, aff = _outproj(ym, ya, x, mod, wo[:d_m], wo[d_m:], norm2_g.reshape(1, D), wr_hi, wr_lo)

    aff_t = aff.transpose(0, 2, 1)
    rank_t, starts = _topk(aff_t, cap)
    xe = _gather(starts, h2, aff_t, rank_t, cap)
    y = _ffn(xe, w_gate_e, w_up_e, w_down_e)
    return _combine(starts, y, rank_t.transpose(0, 2, 1), x1, mod, cap)


def kernel(x, c, positions, norm1_g, norm2_g, w_ada, b_ada, w_in, mlstm_conv_w, mlstm_gate_b, mlstm_norm_g,
           diff_qk_g, diff_lambda, diff_subln_g, w_out, w_router, w_gate_e, w_up_e, w_down_e):
    for l in range(norm1_g.shape[0]):
        x = _layer(l, x, c, positions, norm1_g[l], norm2_g[l], w_ada[l], b_ada[l], w_in[l], mlstm_conv_w[l],
                   mlstm_gate_b[l], mlstm_norm_g[l], diff_qk_g[l], diff_lambda[l], diff_subln_g[l], w_out[l],
                   w_router[l], w_gate_e[l], w_up_e[l], w_down_e[l])
    return x
```

```python
import functools
import math

import numpy as np
import jax
import jax.numpy as jnp
from jax import lax
from jax.experimental import pallas as pl
from jax.experimental.pallas import tpu as pltpu

F32 = jnp.float32
BF16 = jnp.bfloat16
I32 = jnp.int32

M_HEADS = 4
A_HEADS = 4
CONV_K = 5
ROPE_THETA = 500000.0
EPS = 1e-6
CAP_FACTOR = 2

MLSTM_CHUNK = 256
SCAN_UNROLL = 8
ROW_TILE = 512
OUTPROJ_TILE = 1024
OUTPROJ_SUB = 256
ATT_TQ = 1024
ATT_TK = 512
ATT_GROUPS = 8
TOK_TILE = 256
BF16_ROWS = 16
AFF_LANES = 128
GATHER_UNROLL = 2
GATHER_EXPERTS = 4
COMBINE_TILES = 2
SMALL_WIN = 64
FFN_ROWS = 2048
FFN_HIDDEN = 512
FFN_SUB = 512
VMEM_LIMIT = 56 * 1024 * 1024


def _cparams(sem):
    return pltpu.CompilerParams(dimension_semantics=sem, vmem_limit_bytes=VMEM_LIMIT)


def _sigmoid(v):
    return 1.0 / (1.0 + jnp.exp(-v))


def _rms(v):
    return v * lax.rsqrt(jnp.mean(v * v, axis=-1, keepdims=True) + EPS)


def _ada_body(c_ref, w_ref, b_ref, o_ref):
    c = c_ref[...]
    s = (c * _sigmoid(c)).astype(BF16)
    o_ref[...] = jnp.dot(s, w_ref[...].astype(BF16), preferred_element_type=F32) + b_ref[...]


def _ada(c, w_ada, b_ada):
    B, D = c.shape
    N = w_ada.shape[1]
    tn = D
    return pl.pallas_call(
        _ada_body,
        grid=(N // tn,),
        in_specs=[pl.BlockSpec((B, D), lambda j: (0, 0)),
                  pl.BlockSpec((D, tn), lambda j: (0, j)),
                  pl.BlockSpec((1, tn), lambda j: (0, j))],
        out_specs=pl.BlockSpec((B, tn), lambda j: (0, j)),
        out_shape=jax.ShapeDtypeStruct((B, N), F32),
        compiler_params=_cparams(("arbitrary",)),
        name="ada",
    )(c, w_ada, b_ada.reshape(1, N))


def _inproj_body(x_ref, mod_ref, g_ref, wm_ref, wa_ref, wg_ref, bg_ref, pos_ref, invf_ref, fl_ref, s1_ref, s2_ref,
                 gm_ref, gq_ref, gk_ref, pm_ref, qn_ref, kn_ref, v_ref, gt_ref, *, W, q_scale):
    s1 = s1_ref[...]
    s2 = s2_ref[...]
    gm = gm_ref[...]
    lanes = gm.shape[0]
    tm = x_ref.shape[1]

    ang_t = invf_ref[...] * pos_ref[0, 0].astype(F32)
    cs_c = jnp.cos(ang_t).T
    sn_c = jnp.sin(ang_t).T
    fl = fl_ref[...]
    cs = jnp.ones((tm, lanes), F32)
    sn = jnp.zeros((tm, lanes), F32)
    for f in range(invf_ref.shape[0]):
        cs = jnp.where(fl == f, cs_c[:, f:f + 1], cs)
        sn = jnp.where(fl == f, sn_c[:, f:f + 1], sn)

    x = x_ref[0]
    h = _rms(x) * g_ref[...]
    h = (h * (1.0 + mod_ref[0, 1:2, :]) + mod_ref[0, 0:1, :]).astype(BF16)
    pa = jnp.dot(h, wa_ref[...], preferred_element_type=F32)
    v_ref[0] = pa[:, 2 * W:3 * W].astype(BF16)

    def prep(col0, j, gain_ref, dst_ref, scale):
        t = pa[:, col0 + j * lanes:col0 + (j + 1) * lanes]
        ms = jnp.dot((t * t).astype(BF16), gm, preferred_element_type=F32)
        y = t * lax.rsqrt(ms + EPS) * gain_ref[...]
        half = pltpu.roll(y, lanes - 8, 1) * s1 + pltpu.roll(y, 8, 1) * s2
        r = y * cs + half * sn
        dst_ref[0, :, j * lanes:(j + 1) * lanes] = (r * scale).astype(BF16)

    pieces = [(0, j, gq_ref, qn_ref, q_scale) for j in range(W // lanes)] \
        + [(W, j, gk_ref, kn_ref, 1.0) for j in range(W // lanes)]
    cw = pm_ref.shape[2] // len(pieces)
    for i, piece in enumerate(pieces):
        pm_ref[0, :, i * cw:(i + 1) * cw] = jnp.dot(h, wm_ref[:, i * cw:(i + 1) * cw],
                                                     preferred_element_type=F32).astype(BF16)
        prep(*piece)
    gt_ref[0] = jnp.dot(h, wg_ref[...], preferred_element_type=F32) + bg_ref[...]


def _inproj(x, mod, g1, wm, wa, wg, bg, positions, qk_g):
    B, S, D = x.shape
    tm = min(ROW_TILE, S)
    nm, na, ng = wm.shape[1], wa.shape[1], wg.shape[1]
    W = na // 3
    hd = qk_g.shape[1]
    rot = hd // 4
    half = rot // 2
    lanes = 128
    d = np.arange(lanes) % hd
    inv_freq = (ROPE_THETA ** (-np.arange(0, rot, 2, dtype=np.float32) / rot)).astype(np.float32)
    flane = np.where(d < rot, d % half, -1).astype(np.int32)[None, :]
    s1 = np.where(d < half, -1.0, 0.0).astype(np.float32)[None, :]
    s2 = np.where((d >= half) & (d < rot), 1.0, 0.0).astype(np.float32)[None, :]
    gm = (np.arange(lanes)[:, None] // hd == np.arange(lanes)[None, :] // hd).astype(np.float32) / hd
    gq = jnp.tile(qk_g[0], lanes // hd)[None, :]
    gk = jnp.tile(qk_g[1], lanes // hd)[None, :]
    q_scale = (hd ** -0.5) * math.log2(math.e)
    body = functools.partial(_inproj_body, W=W, q_scale=q_scale)
    const = lambda shape: pl.BlockSpec(shape, lambda b, i: (0, 0))
    rows = lambda n: pl.BlockSpec((1, tm, n), lambda b, i: (b, i, 0))
    return pl.pallas_call(
        body,
        grid=(B, S // tm),
        in_specs=[rows(D),
                  pl.BlockSpec((1, 6, D), lambda b, i: (b, 0, 0)),
                  const((1, D)), const((D, nm)), const((D, na)), const((D, ng)), const((1, ng)),
                  pl.BlockSpec((1, 1, 1, tm), lambda b, i: (b, i, 0, 0)),
                  const((half, 1)), const((1, lanes)),
                  const((1, lanes)), const((1, lanes)), const((lanes, lanes)),
                  const((1, lanes)), const((1, lanes))],
        out_specs=[rows(nm), rows(W), rows(W), rows(W), rows(ng)],
        out_shape=[jax.ShapeDtypeStruct((B, S, nm), BF16),
                   jax.ShapeDtypeStruct((B, S, W), BF16),
                   jax.ShapeDtypeStruct((B, S, W), BF16),
                   jax.ShapeDtypeStruct((B, S, W), BF16),
                   jax.ShapeDtypeStruct((B, S, ng), F32)],
        compiler_params=_cparams(("arbitrary", "arbitrary")),
        name="inproj",
    )(x, mod, g1, wm, wa, wg, bg, positions.reshape(B, S // tm, 1, tm), jnp.asarray(inv_freq[:, None]),
      jnp.asarray(flane), jnp.asarray(s1), jnp.asarray(s2), jnp.asarray(gm, dtype=BF16), gq, gk)


def _log_sigmoid(v):
    return jnp.minimum(v, 0.0) - jnp.log(1.0 + jnp.exp(-jnp.abs(v)))


def _mlstm_body(q_ref, k_ref, v_ref, o_ref, cwq_ref, cwk_ref, gt_ref, ng_ref, y_ref,
                xp_s, q_s, kt_s, va_s, hf_s, a_s, *, S, L, dh):
    hd = pl.program_id(1)
    nc = S // L
    win = L + 16

    def conv(src_ref, cw_ref, post):
        xp_s[0:8, :] = jnp.zeros((8, dh), F32)
        xp_s[S + 8:S + 16, :] = jnp.zeros((8, dh), F32)

        def fill(c, _):
            base = pl.multiple_of(c * L, L)
            xp_s[pl.ds(pl.multiple_of(base + 8, 8), L), :] = src_ref[0, pl.ds(base, L), :].astype(F32)
            return 0
        lax.fori_loop(0, nc, fill, 0)

        def body(c, _):
            base = pl.multiple_of(c * L, L)
            acc = None
            for j in range(CONV_K):
                term = xp_s[pl.ds(base + (8 - CONV_K // 2 + j), L), :] * cw_ref[j:j + 1, :]
                acc = term if acc is None else acc + term
            post(base, acc * _sigmoid(acc))
            return 0
        lax.fori_loop(0, nc, body, 0, unroll=min(2, nc))

    def post_q(base, a):
        q_s[pl.ds(base, L), :] = a.astype(BF16)

    def post_k(base, a):
        kt_s[:, pl.ds(base, L)] = (a * (dh ** -0.5)).T.astype(BF16)

    conv(q_ref, cwq_ref, post_q)
    conv(k_ref, cwk_ref, post_k)

    ones_col = (lax.broadcasted_iota(I32, (L, dh), 1) == 0).astype(BF16)

    def fill_v(c, _):
        base = pl.multiple_of(c * L, L)
        va_s[pl.ds(base, L), 0:dh] = v_ref[0, pl.ds(base, L), :]
        va_s[pl.ds(base, L), dh:2 * dh] = ones_col
        return 0
    lax.fori_loop(0, nc, fill_v, 0)

    ri = lax.broadcasted_iota(I32, (L, L), 0)
    ci = lax.broadcasted_iota(I32, (L, L), 1)

    def scan(reverse, gi, gf, emit):
        logf_all = _log_sigmoid(gt_ref[0, gf])
        tri = ((ri >= ci) if reverse else (ri <= ci)).astype(F32)
        a_s[...] = jnp.dot(logf_all, tri, preferred_element_type=F32, precision=lax.Precision.HIGHEST)
        valid = (ci >= ri) if reverse else (ci <= ri)

        def body(step, carry):
            caug, m = carry
            c = (nc - 1 - step) if reverse else step
            base = pl.multiple_of(c * L, L)
            i_row = gt_ref[0, gi, pl.ds(c, 1), :]
            logf = _log_sigmoid(gt_ref[0, gf, pl.ds(c, 1), :])
            a_row = a_s[pl.ds(c, 1), :]
            a_col = jnp.sum(jnp.where(valid, logf, 0.0), axis=1, keepdims=True)
            g = jnp.sum(logf, axis=1, keepdims=True)
            dmat = jnp.where(valid, a_col - a_row + i_row, -jnp.inf)
            dmax = jnp.max(dmat, axis=1, keepdims=True)
            q_c = q_s[pl.ds(base, L), :]
            kt_c = kt_s[:, pl.ds(base, L)]
            va_c = va_s[pl.ds(base, L), :]
            qk = jnp.dot(q_c, kt_c, preferred_element_type=F32)
            p = (qk * jnp.exp(dmat - dmax)).astype(BF16)
            intra = jnp.dot(p, va_c, preferred_element_type=F32)
            inter = a_col + m
            mj = jnp.maximum(inter, dmax)
            nd = (jnp.exp(dmax - mj) * intra
                  + jnp.exp(inter - mj) * jnp.dot(q_c, caug.astype(BF16), preferred_element_type=F32))
            den = jnp.maximum(jnp.abs(nd[:, dh:dh + 1]), jnp.exp(-mj))
            emit(base, nd[:, 0:dh] * (1.0 / den))
            wlog = g - a_row + i_row
            mloc = jnp.max(wlog, axis=1, keepdims=True)
            kw = (kt_c.astype(F32) * jnp.exp(wlog - mloc)).astype(BF16)
            cloc = jnp.dot(kw, va_c, preferred_element_type=F32)
            m_new = jnp.maximum(g + m, mloc)
            caug = jnp.exp(g + m - m_new) * caug + jnp.exp(mloc - m_new) * cloc
            return caug, m_new

        lax.fori_loop(0, nc, body, (jnp.zeros((dh, 2 * dh), F32), jnp.zeros((1, 1), F32)),
                      unroll=min(SCAN_UNROLL, nc))

    def emit_f(base, h):
        hf_s[pl.ds(base, L), :] = h

    def emit_b(base, h):
        hn = _rms(hf_s[pl.ds(base, L), :] + h) * ng_ref[...]
        og = o_ref[0, pl.ds(base, L), :].astype(F32)
        y_ref[0, pl.ds(base, L), :] = (_sigmoid(og) * hn).astype(BF16)

    scan(False, hd, M_HEADS + hd, emit_f)
    scan(True, 2 * M_HEADS + hd, 3 * M_HEADS + hd, emit_b)


def _mlstm(pm, conv_w, gt, norm_g):
    B, S, _ = pm.shape
    dh = norm_g.shape[1] // M_HEADS
    L = min(MLSTM_CHUNK, S)
    nc = S // L
    H = M_HEADS
    body = functools.partial(_mlstm_body, S=S, L=L, dh=dh)
    return pl.pallas_call(
        body,
        grid=(B, H),
        in_specs=[pl.BlockSpec((1, S, dh), lambda b, h: (b, 0, h)),
                  pl.BlockSpec((1, S, dh), lambda b, h: (b, 0, H + h)),
                  pl.BlockSpec((1, S, dh), lambda b, h: (b, 0, 2 * H + h)),
                  pl.BlockSpec((1, S, dh), lambda b, h: (b, 0, 3 * H + h)),
                  pl.BlockSpec((CONV_K, dh), lambda b, h: (0, h)),
                  pl.BlockSpec((CONV_K, dh), lambda b, h: (0, H + h)),
                  pl.BlockSpec((1, 4 * H, nc, L), lambda b, h: (b, 0, 0, 0)),
                  pl.BlockSpec((1, dh), lambda b, h: (0, h))],
        out_specs=pl.BlockSpec((1, S, dh), lambda b, h: (b, 0, h)),
        out_shape=jax.ShapeDtypeStruct((B, S, H * dh), BF16),
        scratch_shapes=[pltpu.VMEM((S + 16, dh), F32),
                        pltpu.VMEM((S, dh), BF16),
                        pltpu.VMEM((dh, S), BF16),
                        pltpu.VMEM((S, 2 * dh), BF16),
                        pltpu.VMEM((S, dh), F32),
                        pltpu.VMEM((nc, L), F32)],
        compiler_params=_cparams(("arbitrary", "arbitrary")),
        name="mlstm",
    )(pm, pm, pm, pm, conv_w, conv_w, gt, norm_g)


def _attn_body(q_ref, k_ref, v_ref, lam_ref, sg_ref, y_ref, va_s, kt_s, s_s, *, S, tq, tk, dv, lam_init):
    nk = S // tk
    lanes = 128

    @pl.when(pl.program_id(2) == 0)
    def _():
        ones_col = (lax.broadcasted_iota(I32, (tk, dv), 1) == 0).astype(BF16)
        for kb in range(nk):
            va_s[kb * tk:(kb + 1) * tk, 0:dv] = v_ref[0, kb * tk:(kb + 1) * tk, :]
            va_s[kb * tk:(kb + 1) * tk, dv:2 * dv] = ones_col
            kt_s[:, kb * tk:(kb + 1) * tk] = k_ref[0, kb * tk:(kb + 1) * tk, :].astype(F32).T.astype(BF16)

    lp = lam_ref[...]
    lam = (jnp.exp(jnp.sum(lp[0:1, :] * lp[1:2, :], axis=1, keepdims=True))
           - jnp.exp(jnp.sum(lp[2:3, :] * lp[3:4, :], axis=1, keepdims=True)) + lam_init)

    q = q_ref[0]
    lane = lax.broadcasted_iota(I32, q.shape, 1)
    hd = q.shape[1] // 2

    qs = jnp.concatenate([jnp.where(lane < hd, q, jnp.zeros_like(q)),
                          jnp.where(lane >= hd, q, jnp.zeros_like(q))], axis=0)
    G = ATT_GROUPS
    rg = 2 * tq // G
    mparts, mbs, accs = [None] * G, [None] * G, [None] * G

    def score(g, kb):
        rows = slice(g * rg, (g + 1) * rg)
        s = jnp.dot(qs[rows, :], kt_s[:, kb * tk:(kb + 1) * tk], preferred_element_type=F32)
        s_s[rows, kb * tk:(kb + 1) * tk] = s
        for j in range(tk // lanes):
            blk = s[:, j * lanes:(j + 1) * lanes]
            mparts[g] = blk if mparts[g] is None else jnp.maximum(mparts[g], blk)

    def rowmax(g):
        mbs[g] = jnp.broadcast_to(jnp.max(mparts[g], axis=1, keepdims=True), (rg, lanes))

    def weigh(g, kb):
        rows = slice(g * rg, (g + 1) * rg)
        p = jnp.concatenate(
            [jnp.exp2(s_s[rows, kb * tk + j * lanes:kb * tk + (j + 1) * lanes] - mbs[g]).astype(BF16)
             for j in range(tk // lanes)], axis=1)
        part = jnp.dot(p, va_s[kb * tk:(kb + 1) * tk, :], preferred_element_type=F32)
        accs[g] = part if accs[g] is None else accs[g] + part

    first, second = range(G // 2), range(G // 2, G)
    for kb in range(nk):
        for g in first:
            score(g, kb)
    for g in first:
        rowmax(g)
    for kb in range(nk):
        for g in second:
            score(g, kb)
        for g in first:
            weigh(g, kb)
    for g in second:
        rowmax(g)
    for kb in range(nk):
        for g in second:
            weigh(g, kb)
    acc = jnp.concatenate(accs, axis=0)
    r = acc[:, 0:dv] * (1.0 / acc[:, dv:dv + 1])
    o = r[0:tq, :] - lam * r[tq:2 * tq, :]
    y_ref[0] = (_rms(o) * sg_ref[...] * (1.0 - lam_init)).astype(BF16)


def _attn(qn, kn, va, lam_p, subln_g, layer_idx):
    B, S, W = qn.shape
    dv = W // A_HEADS
    tq = min(ATT_TQ, S)
    tk = min(ATT_TK, S)
    lam_init = 0.8 - 0.6 * math.exp(-0.3 * layer_idx)
    body = functools.partial(_attn_body, S=S, tq=tq, tk=tk, dv=dv, lam_init=lam_init)
    return pl.pallas_call(
        body,
        grid=(B, A_HEADS, S // tq),
        in_specs=[pl.BlockSpec((1, tq, dv), lambda b, h, i: (b, i, h)),
                  pl.BlockSpec((1, S, dv), lambda b, h, i: (b, 0, h)),
                  pl.BlockSpec((1, S, dv), lambda b, h, i: (b, 0, h)),
                  pl.BlockSpec(lam_p.shape, lambda b, h, i: (0, 0)),
                  pl.BlockSpec((1, dv), lambda b, h, i: (0, 0))],
        out_specs=pl.BlockSpec((1, tq, dv), lambda b, h, i: (b, i, h)),
        out_shape=jax.ShapeDtypeStruct((B, S, W), BF16),
        scratch_shapes=[pltpu.VMEM((S, 2 * dv), BF16), pltpu.VMEM((dv, S), BF16), pltpu.VMEM((2 * tq, S), F32)],
        compiler_params=_cparams(("arbitrary", "arbitrary", "arbitrary")),
        name="attn",
    )(qn, kn, va, lam_p, subln_g.reshape(1, dv))


def _outproj_body(ym_ref, ya_ref, x_ref, mod_ref, wom_ref, woa_ref, g2_ref, wrh_ref, wrc_ref,
                  x1_ref, h2_ref, aff_ref, *, sub):
    for r in range(x_ref.shape[1] // sub):
        rows = slice(r * sub, (r + 1) * sub)
        mix = (jnp.dot(ym_ref[0, rows, :], wom_ref[...], preferred_element_type=F32)
               + jnp.dot(ya_ref[0, rows, :], woa_ref[...], preferred_element_type=F32))
        x1 = x_ref[0, rows, :] + mod_ref[0, 2:3, :] * mix
        x1_ref[0, rows, :] = x1
        h2 = _rms(x1) * g2_ref[...] * (1.0 + mod_ref[0, 4:5, :]) + mod_ref[0, 3:4, :]
        hh = h2.astype(BF16)
        h2_ref[0, rows, :] = hh
        hl = (h2 - hh.astype(F32)).astype(BF16)
        E = wrh_ref.shape[1]
        both = jnp.dot(hh, wrc_ref[...], preferred_element_type=F32)
        logits = both[:, 0:E] + both[:, E:2 * E] + jnp.dot(hl, wrh_ref[...], preferred_element_type=F32)
        ex = jnp.exp(logits - jnp.max(logits, axis=1, keepdims=True))
        aff_ref[0, rows, :] = ex * (1.0 / jnp.sum(ex, axis=1, keepdims=True))


def _outproj(ym, ya, x, mod, wom, woa, g2, wrh, wrc):
    B, S, D = x.shape
    tm = min(OUTPROJ_TILE, S)
    wm, wa, E = ym.shape[2], ya.shape[2], wrh.shape[1]
    full = lambda shape: pl.BlockSpec(shape, lambda b, i: (0, 0))
    return pl.pallas_call(
        functools.partial(_outproj_body, sub=min(OUTPROJ_SUB, tm)),
        grid=(B, S // tm),
        in_specs=[pl.BlockSpec((1, tm, wm), lambda b, i: (b, i, 0)),
                  pl.BlockSpec((1, tm, wa), lambda b, i: (b, i, 0)),
                  pl.BlockSpec((1, tm, D), lambda b, i: (b, i, 0)),
                  pl.BlockSpec((1, 6, D), lambda b, i: (b, 0, 0)),
                  full((wm, D)), full((wa, D)), full((1, D)), full((D, E)), full((D, 2 * E))],
        out_specs=[pl.BlockSpec((1, tm, D), lambda b, i: (b, i, 0)),
                   pl.BlockSpec((1, tm, D), lambda b, i: (b, i, 0)),
                   pl.BlockSpec((1, tm, E), lambda b, i: (b, i, 0))],
        out_shape=[jax.ShapeDtypeStruct((B, S, D), F32),
                   jax.ShapeDtypeStruct((B, S, D), BF16),
                   jax.ShapeDtypeStruct((B, S, E), F32)],
        compiler_params=_cparams(("arbitrary", "arbitrary")),
        name="outproj",
    )(ym, ya, x, mod, wom, woa, g2, wrh, wrc)


def _topk_body(a_ref, rank_ref, starts_ref, *, S, cap, tb):
    bits = lax.bitcast_convert_type(a_ref[0], I32)
    E = bits.shape[0]

    def search(i, thr):
        cand = thr | jnp.left_shift(jnp.int32(1), 30 - i)
        cnt = jnp.sum((bits >= cand).astype(I32), axis=1, keepdims=True)
        return jnp.where(cnt >= cap, cand, thr)
    thr = lax.fori_loop(0, 31, search, jnp.zeros((E, 1), I32))

    gt = bits > thr
    eq = bits == thr
    need = cap - jnp.sum(gt.astype(I32), axis=1, keepdims=True)

    nt = S // tb
    upper = (lax.broadcasted_iota(I32, (tb, tb), 0) < lax.broadcasted_iota(I32, (tb, tb), 1)).astype(BF16)

    def excl_cumsum(mask, with_starts):
        off = jnp.zeros((E, 1), F32)
        outs, starts = [], []
        for t in range(nt):
            blk = mask[:, t * tb:(t + 1) * tb].astype(F32).astype(BF16)
            outs.append(jnp.dot(blk, upper, preferred_element_type=F32) + off)
            starts.append(off)
            off = off + jnp.sum(blk.astype(F32), axis=1, keepdims=True)
        res = jnp.concatenate(outs, axis=1).astype(I32)
        return (res, jnp.concatenate(starts, axis=1).astype(I32)) if with_starts else res

    sel = gt | (eq & (excl_cumsum(eq, False) < need))
    rank, starts = excl_cumsum(sel, True)
    rank_ref[0] = jnp.where(sel, rank, -1)
    starts_ref[0] = starts


def _topk(aff_t, cap):
    B, E, S = aff_t.shape
    tb = min(TOK_TILE, S)
    nt = S // tb
    body = functools.partial(_topk_body, S=S, cap=cap, tb=tb)
    return pl.pallas_call(
        body,
        grid=(B,),
        in_specs=[pl.BlockSpec((1, E, S), lambda b: (b, 0, 0))],
        out_specs=[pl.BlockSpec((1, E, S), lambda b: (b, 0, 0)),
                   pl.BlockSpec((1, E, nt), lambda b: (b, 0, 0))],
        out_shape=[jax.ShapeDtypeStruct((B, E, S), I32), jax.ShapeDtypeStruct((B, E, nt), I32)],
        compiler_params=_cparams(("arbitrary",)),
        name="topk",
    )(aff_t)


def _gather_body(starts_ref, h2_ref, aff_ref, rank_ref, xe_ref, acc_s, *, S, E, EG, cap, tb):
    b = pl.program_id(0)
    g = pl.program_id(1)
    nt = S // tb
    D = h2_ref.shape[2]
    DX = acc_s.shape[2]
    big = tb + BF16_ROWS
    sm = min(SMALL_WIN, big)
    head = lax.broadcasted_iota(I32, (BF16_ROWS, DX), 0)

    def bounds(j, t):
        row = (b * E + g * EG + j) * nt
        r0 = starts_ref[row + t]
        r_end = jnp.where(t == nt - 1, cap, starts_ref[row + jnp.minimum(t + 1, nt - 1)])
        r0a = pl.multiple_of((r0 // BF16_ROWS) * BF16_ROWS, BF16_ROWS)
        return r0, r_end, r0a

    def put(j, r0, r0a, rows, hit, off, w):
        a = jnp.sum(jnp.where(hit, aff_ref[0, j, :, pl.ds(off, tb)], 0.0), axis=1, keepdims=True)
        a1 = a.astype(BF16).astype(F32)
        a2 = (a - a1).astype(BF16).astype(F32)
        a3 = a - a1 - a2
        xlane = lax.broadcasted_iota(I32, (w, AFF_LANES), 1)
        ext = jnp.where(xlane == 0, a1, jnp.where(xlane == 1, a2, jnp.where(xlane == 2, a3, 0.0))).astype(BF16)
        full = jnp.concatenate([rows, ext], axis=1)
        keep = acc_s[j, pl.ds(r0a, BF16_ROWS), :]
        acc_s[j, pl.ds(r0a, BF16_ROWS), :] = jnp.where(head < r0 - r0a, keep, full[0:BF16_ROWS, :])
        acc_s[j, pl.ds(r0a + BF16_ROWS, w - BF16_ROWS), :] = full[BF16_ROWS:, :]

    def tile(t, _):
        off = pl.multiple_of(t * tb, tb)
        bnds = [bounds(j, t) for j in range(EG)]
        fits = None
        for r0, r_end, r0a in bnds:
            f = r_end - r0a < sm
            fits = f if fits is None else (fits & f)

        @pl.when(fits)
        def _():
            sub = lax.broadcasted_iota(I32, (sm, tb), 0)
            hits = [sub == rank_ref[0, j, :, pl.ds(off, tb)] - bnds[j][2] for j in range(EG)]
            onehot = jnp.concatenate([h.astype(F32) for h in hits], axis=0).astype(BF16)
            rows = jnp.dot(onehot, h2_ref[0, pl.ds(off, tb), :], preferred_element_type=F32).astype(BF16)
            for j in range(EG):
                put(j, bnds[j][0], bnds[j][2], rows[j * sm:(j + 1) * sm, :], hits[j], off, sm)

        @pl.when(jnp.logical_not(fits))
        def _():
            sub = lax.broadcasted_iota(I32, (big, tb), 0)
            for j in range(EG):
                hit = sub == rank_ref[0, j, :, pl.ds(off, tb)] - bnds[j][2]
                rows = jnp.dot(hit.astype(F32).astype(BF16), h2_ref[0, pl.ds(off, tb), :],
                               preferred_element_type=F32).astype(BF16)
                put(j, bnds[j][0], bnds[j][2], rows, hit, off, big)
        return 0

    for j in range(EG):
        acc_s[j, 0:BF16_ROWS, :] = jnp.zeros((BF16_ROWS, DX), BF16)
    lax.fori_loop(0, nt, tile, 0, unroll=min(GATHER_UNROLL, nt))
    for j in range(EG):
        xe_ref[j] = acc_s[j, 0:cap, :]


def _gather(starts, h2, aff_t, rank_t, cap):
    B, S, D = h2.shape
    E = rank_t.shape[1]
    EG = min(GATHER_EXPERTS, E)
    tb = min(TOK_TILE, S)
    DX = D + AFF_LANES
    body = functools.partial(_gather_body, S=S, E=E, EG=EG, cap=cap, tb=tb)
    grid_spec = pltpu.PrefetchScalarGridSpec(
        num_scalar_prefetch=1,
        grid=(B, E // EG),
        in_specs=[pl.BlockSpec((1, S, D), lambda b, g, st: (b, 0, 0)),
                  pl.BlockSpec((1, EG, 1, S), lambda b, g, st: (b, g, 0, 0)),
                  pl.BlockSpec((1, EG, 1, S), lambda b, g, st: (b, g, 0, 0))],
        out_specs=pl.BlockSpec((EG, cap, DX), lambda b, g, st: (g, b, 0)),
        scratch_shapes=[pltpu.VMEM((EG, cap + tb + BF16_ROWS, DX), BF16)],
    )
    return pl.pallas_call(
        body,
        grid_spec=grid_spec,
        out_shape=jax.ShapeDtypeStruct((E, B * cap, DX), BF16),
        compiler_params=_cparams(("arbitrary", "arbitrary")),
        name="gather",
    )(starts.reshape(-1), h2, aff_t.reshape(B, E, 1, S), rank_t.reshape(B, E, 1, S))


def _ffn_body(x_ref, wg_ref, wu_ref, wd_ref, y_ref, acc_s, *, tt, sub, nfc):
    fc = pl.program_id(2)
    D = y_ref.shape[2]

    def mm(a, w):
        return lax.dot_general(a, w, (((1,), (0,)), ((), ())), preferred_element_type=F32)

    if nfc > 1:
        @pl.when((pl.program_id(0) == 0) & (pl.program_id(1) == 0) & (fc == 0))
        def _():
            acc_s[...] = jnp.zeros(acc_s.shape, F32)

    for i in range(tt // sub):
        rows = slice(i * sub, (i + 1) * sub)
        x = x_ref[0, rows, 0:D]
        g = mm(x, wg_ref[0])
        u = mm(x, wu_ref[0])
        a = (g * _sigmoid(g) * u).astype(BF16)
        total = mm(a, wd_ref[0])
        if nfc > 1:
            total = jnp.where(fc == 0, total, acc_s[rows, :] + total)
            acc_s[rows, :] = total
        val = jnp.sum(x_ref[0, rows, D:].astype(F32), axis=1, keepdims=True)
        y_ref[0, rows, :] = (total * val).astype(BF16)


def _ffn(xe, w_gate, w_up, w_down):
    E, T, DX = xe.shape
    D = w_gate.shape[1]
    F = w_gate.shape[2]
    tt = min(FFN_ROWS, T)
    tf = min(FFN_HIDDEN, F)
    sub = min(FFN_SUB, tt)
    body = functools.partial(_ffn_body, tt=tt, sub=sub, nfc=F // tf)
    return pl.pallas_call(
        body,
        grid=(E, T // tt, F // tf),
        in_specs=[pl.BlockSpec((1, tt, DX), lambda e, t, f: (e, t, 0)),
                  pl.BlockSpec((1, D, tf), lambda e, t, f: (e, 0, f)),
                  pl.BlockSpec((1, D, tf), lambda e, t, f: (e, 0, f)),
                  pl.BlockSpec((1, tf, D), lambda e, t, f: (e, f, 0))],
        out_specs=pl.BlockSpec((1, tt, D), lambda e, t, f: (e, t, 0)),
        out_shape=jax.ShapeDtypeStruct((E, T, D), BF16),
        scratch_shapes=[pltpu.VMEM((tt, D), F32)],
        compiler_params=_cparams(("arbitrary", "arbitrary", "arbitrary")),
        name="ffn",
    )(xe, w_gate, w_up, w_down)


def _combine_body(starts_ref, y_ref, rank_ref, x1_ref, mod_ref, o_ref, ycat_s, *, S, E, cap, tb):
    b = pl.program_id(0)
    nt = S // tb
    nsub = o_ref.shape[1] // tb
    kw = min(tb, cap)
    sm = min(SMALL_WIN, kw)
    lanes = 128
    per = lanes // sm
    gate = mod_ref[0, 5:6, :]

    def window(u, e, w):
        t = pl.program_id(1) * nsub + u
        row = (b * E + e) * nt
        r0 = starts_ref[row + t]
        r_end = jnp.where(t == nt - 1, cap, starts_ref[row + jnp.minimum(t + 1, nt - 1)])
        r0a = pl.multiple_of(jnp.minimum((r0 // BF16_ROWS) * BF16_ROWS, cap - w), BF16_ROWS)
        return r0a, r_end > r0a + w

    fits = None
    for u in range(nsub):
        for e in range(E):
            f = jnp.logical_not(window(u, e, sm)[1])
            fits = f if fits is None else (fits & f)

    @pl.when(fits)
    def _():
        lane = lax.broadcasted_iota(I32, (tb, lanes), 1)
        for u in range(nsub):
            rows = slice(u * tb, (u + 1) * tb)
            rank = rank_ref[0, rows, :]
            tiles = []
            for k in range(E // per):
                hit = None
                for i in range(per):
                    e = k * per + i
                    r0a, _ = window(u, e, sm)
                    ycat_s[u, e * sm:(e + 1) * sm, :] = y_ref[e, pl.ds(r0a, sm), :]
                    col = rank[:, e:e + 1] - r0a
                    tgt = jnp.where((col >= 0) & (col < sm), col + i * sm, -1)
                    h = lane == tgt
                    hit = h if hit is None else (hit | h)
                tiles.append(hit.astype(F32).astype(BF16))
            onehot = jnp.concatenate(tiles, axis=1)
            o_ref[0, rows, :] = x1_ref[0, rows, :] + gate * jnp.dot(onehot, ycat_s[u],
                                                                   preferred_element_type=F32)

    @pl.when(jnp.logical_not(fits))
    def _():
        lane = lax.broadcasted_iota(I32, (tb, kw), 1)
        for u in range(nsub):
            rows = slice(u * tb, (u + 1) * tb)
            rank = rank_ref[0, rows, :]
            acc = None
            for e in range(E):
                r0a, _ = window(u, e, kw)
                onehot = (rank[:, e:e + 1] - r0a == lane).astype(F32).astype(BF16)
                part = jnp.dot(onehot, y_ref[e, pl.ds(r0a, kw), :], preferred_element_type=F32)
                acc = part if acc is None else acc + part
            o_ref[0, rows, :] = x1_ref[0, rows, :] + gate * acc

            if cap > kw:
                for e in range(E):
                    r0a, over = window(u, e, kw)

                    @pl.when(over)
                    def _(e=e, r0a=r0a, rows=rows):
                        r1a = pl.multiple_of(jnp.minimum(r0a + kw, cap - kw), BF16_ROWS)
                        rk = rank_ref[0, rows, e:e + 1]
                        oh = ((rk - r1a == lane) & (rk >= r0a + kw)).astype(F32).astype(BF16)
                        o_ref[0, rows, :] += gate * jnp.dot(oh, y_ref[e, pl.ds(r1a, kw), :],
                                                            preferred_element_type=F32)


def _combine(starts, y, rank, x1, mod, cap):
    B, S, D = x1.shape
    E = rank.shape[2]
    tb = min(TOK_TILE, S)
    nsub = min(COMBINE_TILES, S // tb)
    rows = nsub * tb
    body = functools.partial(_combine_body, S=S, E=E, cap=cap, tb=tb)
    grid_spec = pltpu.PrefetchScalarGridSpec(
        num_scalar_prefetch=1,
        grid=(B, S // rows),
        in_specs=[pl.BlockSpec((E, cap, D), lambda b, t, st: (0, b, 0)),
                  pl.BlockSpec((1, rows, E), lambda b, t, st: (b, t, 0)),
                  pl.BlockSpec((1, rows, D), lambda b, t, st: (b, t, 0)),
                  pl.BlockSpec((1, 6, D), lambda b, t, st: (b, 0, 0))],
        out_specs=pl.BlockSpec((1, rows, D), lambda b, t, st: (b, t, 0)),
        scratch_shapes=[pltpu.VMEM((nsub, E * min(SMALL_WIN, tb, cap), D), BF16)],
    )
    return pl.pallas_call(
        body,
        grid_spec=grid_spec,
        out_shape=jax.ShapeDtypeStruct((B, S, D), F32),
        compiler_params=_cparams(("arbitrary", "arbitrary")),
        name="combine",
    )(starts.reshape(-1), y, rank, x1, mod)


def _layer(layer_idx, x, c, positions, norm1_g, norm2_g, w_ada, b_ada, w_in, conv_w, gate_b, mnorm_g, qk_g,
           lam_p, subln_g, w_out, w_router, w_gate_e, w_up_e, w_down_e):
    B, S, D = x.shape
    E = w_router.shape[1]
    d_m = mnorm_g.shape[0]
    n_g = 4 * M_HEADS
    d_a = w_out.shape[0] - d_m
    cap = CAP_FACTOR * S // E
    L = min(MLSTM_CHUNK, S)

    mod = _ada(c, w_ada, b_ada).reshape(B, 6, D)

    wm = w_in[:, :4 * d_m].astype(BF16)
    wg = w_in[:, 4 * d_m:4 * d_m + n_g].astype(BF16)
    wa = w_in[:, 4 * d_m + n_g:].astype(BF16)
    pm, qn, kn, va, gates = _inproj(x, mod, norm1_g.reshape(1, D), wm, wa, wg, gate_b.reshape(1, n_g),
                                    positions, qk_g)

    gt = gates.reshape(B, S // L, L, n_g).transpose(0, 3, 1, 2)
    ym = _mlstm(pm, conv_w, gt, mnorm_g.reshape(1, d_m))
    ya = _attn(qn, kn, va, lam_p, subln_g, layer_idx)

    wo = w_out.astype(BF16)
    wr_hi = w_router.astype(BF16)
    wr_lo = (w_router - wr_hi.astype(F32)).astype(BF16)
    x1, h2, aff = _outproj(ym, ya, x, mod, wo[:d_m], wo[d_m:], norm2_g.reshape(1, D), wr_hi,
                           jnp.concatenate([wr_hi, wr_lo], axis=1))

    aff_t = aff.transpose(0, 2, 1)
    rank_t, starts = _topk(aff_t, cap)
    xe = _gather(starts, h2, aff_t, rank_t, cap)
    y = _ffn(xe, w_gate_e, w_up_e, w_down_e)
    return _combine(starts, y, rank_t.transpose(0, 2, 1), x1, mod, cap)


def kernel(x, c, positions, norm1_g, norm2_g, w_ada, b_ada, w_in, mlstm_conv_w, mlstm_gate_b, mlstm_norm_g,
           diff_qk_g, diff_lambda, diff_subln_g, w_out, w_router, w_gate_e, w_up_e, w_down_e):
    for l in range(norm1_g.shape[0]):
        x = _layer(l, x, c, positions, norm1_g[l], norm2_g[l], w_ada[l], b_ada[l], w_in[l], mlstm_conv_w[l],
                   mlstm_gate_b[l], mlstm_norm_g[l], diff_qk_g[l], diff_lambda[l], diff_subln_g[l], w_out[l],
                   w_router[l], w_gate_e[l], w_up_e[l], w_down_e[l])
    return x
```

```python
import functools
import math

import numpy as np
import jax
import jax.numpy as jnp
from jax import lax
from jax.experimental import pallas as pl
from jax.experimental.pallas import tpu as pltpu

F32 = jnp.float32
BF16 = jnp.bfloat16
I32 = jnp.int32

M_HEADS = 4
A_HEADS = 4
CONV_K = 5
ROPE_THETA = 500000.0
EPS = 1e-6
CAP_FACTOR = 2

MLSTM_CHUNK = 256
SCAN_UNROLL = 8
ROW_TILE = 512
OUTPROJ_TILE = 1024
OUTPROJ_SUB = 256
ATT_TQ = 1024
ATT_TK = 1024
ATT_GROUPS = 8
TOK_TILE = 256
BF16_ROWS = 16
AFF_LANES = 128
GATHER_TILES = 4
GATHER_EXPERTS = 4
COMBINE_TILES = 2
SMALL_WIN = 64
FFN_ROWS = 2048
FFN_HIDDEN = 512
FFN_SUB = 512
VMEM_LIMIT = 56 * 1024 * 1024


def _cparams(sem):
    return pltpu.CompilerParams(dimension_semantics=sem, vmem_limit_bytes=VMEM_LIMIT)


def _sigmoid(v):
    return 1.0 / (1.0 + jnp.exp(-v))


def _rms(v):
    return v * lax.rsqrt(jnp.mean(v * v, axis=-1, keepdims=True) + EPS)


def _ada_body(c_ref, w_ref, b_ref, o_ref):
    c = c_ref[...]
    s = (c * _sigmoid(c)).astype(BF16)
    o_ref[...] = jnp.dot(s, w_ref[...].astype(BF16), preferred_element_type=F32) + b_ref[...]


def _ada(c, w_ada, b_ada):
    B, D = c.shape
    N = w_ada.shape[1]
    tn = D
    return pl.pallas_call(
        _ada_body,
        grid=(N // tn,),
        in_specs=[pl.BlockSpec((B, D), lambda j: (0, 0)),
                  pl.BlockSpec((D, tn), lambda j: (0, j)),
                  pl.BlockSpec((1, tn), lambda j: (0, j))],
        out_specs=pl.BlockSpec((B, tn), lambda j: (0, j)),
        out_shape=jax.ShapeDtypeStruct((B, N), F32),
        compiler_params=_cparams(("arbitrary",)),
        name="ada",
    )(c, w_ada, b_ada.reshape(1, N))


def _inproj_body(x_ref, mod_ref, g_ref, wm_ref, wa_ref, wg_ref, bg_ref, pos_ref, invf_ref, fl_ref, s1_ref, s2_ref,
                 gm_ref, gq_ref, gk_ref, pm_ref, qn_ref, kn_ref, v_ref, gt_ref, *, W, q_scale):
    s1 = s1_ref[...]
    s2 = s2_ref[...]
    gm = gm_ref[...]
    lanes = gm.shape[0]
    tm = x_ref.shape[1]

    ang_t = invf_ref[...] * pos_ref[0, 0].astype(F32)
    cs_c = jnp.cos(ang_t).T
    sn_c = jnp.sin(ang_t).T
    fl = fl_ref[...]
    cs = jnp.ones((tm, lanes), F32)
    sn = jnp.zeros((tm, lanes), F32)
    for f in range(invf_ref.shape[0]):
        cs = jnp.where(fl == f, cs_c[:, f:f + 1], cs)
        sn = jnp.where(fl == f, sn_c[:, f:f + 1], sn)

    x = x_ref[0]
    h = _rms(x) * g_ref[...]
    h = (h * (1.0 + mod_ref[0, 1:2, :]) + mod_ref[0, 0:1, :]).astype(BF16)
    pa = jnp.dot(h, wa_ref[...], preferred_element_type=F32)
    v_ref[0] = pa[:, 2 * W:3 * W].astype(BF16)

    def prep(col0, j, gain_ref, dst_ref, scale):
        t = pa[:, col0 + j * lanes:col0 + (j + 1) * lanes]
        ms = jnp.dot((t * t).astype(BF16), gm, preferred_element_type=F32)
        y = t * lax.rsqrt(ms + EPS) * gain_ref[...]
        half = pltpu.roll(y, lanes - 8, 1) * s1 + pltpu.roll(y, 8, 1) * s2
        r = y * cs + half * sn
        dst_ref[0, :, j * lanes:(j + 1) * lanes] = (r * scale).astype(BF16)

    pieces = [(0, j, gq_ref, qn_ref, q_scale) for j in range(W // lanes)] \
        + [(W, j, gk_ref, kn_ref, 1.0) for j in range(W // lanes)]
    cw = pm_ref.shape[2] // len(pieces)
    for i, piece in enumerate(pieces):
        pm_ref[0, :, i * cw:(i + 1) * cw] = jnp.dot(h, wm_ref[:, i * cw:(i + 1) * cw],
                                                     preferred_element_type=F32).astype(BF16)
        prep(*piece)
    gt_ref[0] = jnp.dot(h, wg_ref[...], preferred_element_type=F32) + bg_ref[...]


def _inproj(x, mod, g1, wm, wa, wg, bg, positions, qk_g):
    B, S, D = x.shape
    tm = min(ROW_TILE, S)
    nm, na, ng = wm.shape[1], wa.shape[1], wg.shape[1]
    W = na // 3
    hd = qk_g.shape[1]
    rot = hd // 4
    half = rot // 2
    lanes = 128
    d = np.arange(lanes) % hd
    inv_freq = (ROPE_THETA ** (-np.arange(0, rot, 2, dtype=np.float32) / rot)).astype(np.float32)
    flane = np.where(d < rot, d % half, -1).astype(np.int32)[None, :]
    s1 = np.where(d < half, -1.0, 0.0).astype(np.float32)[None, :]
    s2 = np.where((d >= half) & (d < rot), 1.0, 0.0).astype(np.float32)[None, :]
    gm = (np.arange(lanes)[:, None] // hd == np.arange(lanes)[None, :] // hd).astype(np.float32) / hd
    gq = jnp.tile(qk_g[0], lanes // hd)[None, :]
    gk = jnp.tile(qk_g[1], lanes // hd)[None, :]
    q_scale = (hd ** -0.5) * math.log2(math.e)
    body = functools.partial(_inproj_body, W=W, q_scale=q_scale)
    const = lambda shape: pl.BlockSpec(shape, lambda b, i: (0, 0))
    rows = lambda n: pl.BlockSpec((1, tm, n), lambda b, i: (b, i, 0))
    return pl.pallas_call(
        body,
        grid=(B, S // tm),
        in_specs=[rows(D),
                  pl.BlockSpec((1, 6, D), lambda b, i: (b, 0, 0)),
                  const((1, D)), const((D, nm)), const((D, na)), const((D, ng)), const((1, ng)),
                  pl.BlockSpec((1, 1, 1, tm), lambda b, i: (b, i, 0, 0)),
                  const((half, 1)), const((1, lanes)),
                  const((1, lanes)), const((1, lanes)), const((lanes, lanes)),
                  const((1, lanes)), const((1, lanes))],
        out_specs=[rows(nm), rows(W), rows(W), rows(W), rows(ng)],
        out_shape=[jax.ShapeDtypeStruct((B, S, nm), BF16),
                   jax.ShapeDtypeStruct((B, S, W), BF16),
                   jax.ShapeDtypeStruct((B, S, W), BF16),
                   jax.ShapeDtypeStruct((B, S, W), BF16),
                   jax.ShapeDtypeStruct((B, S, ng), F32)],
        compiler_params=_cparams(("arbitrary", "arbitrary")),
        name="inproj",
    )(x, mod, g1, wm, wa, wg, bg, positions.reshape(B, S // tm, 1, tm), jnp.asarray(inv_freq[:, None]),
      jnp.asarray(flane), jnp.asarray(s1), jnp.asarray(s2), jnp.asarray(gm, dtype=BF16), gq, gk)


def _log_sigmoid(v):
    return jnp.minimum(v, 0.0) - jnp.log(1.0 + jnp.exp(-jnp.abs(v)))


def _mlstm_body(q_ref, k_ref, v_ref, o_ref, cwq_ref, cwk_ref, gt_ref, ng_ref, y_ref,
                xp_s, q_s, kt_s, va_s, hf_s, a_s, *, S, L, dh):
    hd = pl.program_id(1)
    nc = S // L
    win = L + 16

    def conv(src_ref, cw_ref, post):
        xp_s[0:8, :] = jnp.zeros((8, dh), F32)
        xp_s[S + 8:S + 16, :] = jnp.zeros((8, dh), F32)

        def fill(c, _):
            base = pl.multiple_of(c * L, L)
            xp_s[pl.ds(pl.multiple_of(base + 8, 8), L), :] = src_ref[0, pl.ds(base, L), :].astype(F32)
            return 0
        lax.fori_loop(0, nc, fill, 0)

        def body(c, _):
            base = pl.multiple_of(c * L, L)
            acc = None
            for j in range(CONV_K):
                term = xp_s[pl.ds(base + (8 - CONV_K // 2 + j), L), :] * cw_ref[j:j + 1, :]
                acc = term if acc is None else acc + term
            post(base, acc * _sigmoid(acc))
            return 0
        lax.fori_loop(0, nc, body, 0, unroll=min(2, nc))

    def post_q(base, a):
        q_s[pl.ds(base, L), :] = a.astype(BF16)

    def post_k(base, a):
        kt_s[:, pl.ds(base, L)] = (a * (dh ** -0.5)).T.astype(BF16)

    conv(q_ref, cwq_ref, post_q)
    conv(k_ref, cwk_ref, post_k)

    ones_col = (lax.broadcasted_iota(I32, (L, dh), 1) == 0).astype(BF16)

    def fill_v(c, _):
        base = pl.multiple_of(c * L, L)
        va_s[pl.ds(base, L), 0:dh] = v_ref[0, pl.ds(base, L), :]
        va_s[pl.ds(base, L), dh:2 * dh] = ones_col
        return 0
    lax.fori_loop(0, nc, fill_v, 0)

    ri = lax.broadcasted_iota(I32, (L, L), 0)
    ci = lax.broadcasted_iota(I32, (L, L), 1)

    def scan(reverse, gi, gf, emit):
        logf_all = _log_sigmoid(gt_ref[0, gf])
        tri = ((ri >= ci) if reverse else (ri <= ci)).astype(F32)
        a_s[...] = jnp.dot(logf_all, tri, preferred_element_type=F32, precision=lax.Precision.HIGHEST)
        valid = (ci >= ri) if reverse else (ci <= ri)

        def body(step, carry):
            caug, m = carry
            c = (nc - 1 - step) if reverse else step
            base = pl.multiple_of(c * L, L)
            i_row = gt_ref[0, gi, pl.ds(c, 1), :]
            logf = _log_sigmoid(gt_ref[0, gf, pl.ds(c, 1), :])
            a_row = a_s[pl.ds(c, 1), :]
            a_col = jnp.sum(jnp.where(valid, logf, 0.0), axis=1, keepdims=True)
            g = jnp.sum(logf, axis=1, keepdims=True)
            dmat = jnp.where(valid, a_col - a_row + i_row, -jnp.inf)
            dmax = jnp.max(dmat, axis=1, keepdims=True)
            q_c = q_s[pl.ds(base, L), :]
            kt_c = kt_s[:, pl.ds(base, L)]
            va_c = va_s[pl.ds(base, L), :]
            qk = jnp.dot(q_c, kt_c, preferred_element_type=F32)
            p = (qk * jnp.exp(dmat - dmax)).astype(BF16)
            intra = jnp.dot(p, va_c, preferred_element_type=F32)
            inter = a_col + m
            mj = jnp.maximum(inter, dmax)
            nd = (jnp.exp(dmax - mj) * intra
                  + jnp.exp(inter - mj) * jnp.dot(q_c, caug.astype(BF16), preferred_element_type=F32))
            den = jnp.maximum(jnp.abs(nd[:, dh:dh + 1]), jnp.exp(-mj))
            emit(base, nd[:, 0:dh] * (1.0 / den))
            wlog = g - a_row + i_row
            mloc = jnp.max(wlog, axis=1, keepdims=True)
            kw = (kt_c.astype(F32) * jnp.exp(wlog - mloc)).astype(BF16)
            cloc = jnp.dot(kw, va_c, preferred_element_type=F32)
            m_new = jnp.maximum(g + m, mloc)
            caug = jnp.exp(g + m - m_new) * caug + jnp.exp(mloc - m_new) * cloc
            return caug, m_new

        lax.fori_loop(0, nc, body, (jnp.zeros((dh, 2 * dh), F32), jnp.zeros((1, 1), F32)),
                      unroll=min(SCAN_UNROLL, nc))

    def emit_f(base, h):
        hf_s[pl.ds(base, L), :] = h

    def emit_b(base, h):
        hn = _rms(hf_s[pl.ds(base, L), :] + h) * ng_ref[...]
        og = o_ref[0, pl.ds(base, L), :].astype(F32)
        y_ref[0, pl.ds(base, L), :] = (_sigmoid(og) * hn).astype(BF16)

    scan(False, hd, M_HEADS + hd, emit_f)
    scan(True, 2 * M_HEADS + hd, 3 * M_HEADS + hd, emit_b)


def _mlstm(pm, conv_w, gt, norm_g):
    B, S, _ = pm.shape
    dh = norm_g.shape[1] // M_HEADS
    L = min(MLSTM_CHUNK, S)
    nc = S // L
    H = M_HEADS
    body = functools.partial(_mlstm_body, S=S, L=L, dh=dh)
    return pl.pallas_call(
        body,
        grid=(B, H),
        in_specs=[pl.BlockSpec((1, S, dh), lambda b, h: (b, 0, h)),
                  pl.BlockSpec((1, S, dh), lambda b, h: (b, 0, H + h)),
                  pl.BlockSpec((1, S, dh), lambda b, h: (b, 0, 2 * H + h)),
                  pl.BlockSpec((1, S, dh), lambda b, h: (b, 0, 3 * H + h)),
                  pl.BlockSpec((CONV_K, dh), lambda b, h: (0, h)),
                  pl.BlockSpec((CONV_K, dh), lambda b, h: (0, H + h)),
                  pl.BlockSpec((1, 4 * H, nc, L), lambda b, h: (b, 0, 0, 0)),
                  pl.BlockSpec((1, dh), lambda b, h: (0, h))],
        out_specs=pl.BlockSpec((1, S, dh), lambda b, h: (b, 0, h)),
        out_shape=jax.ShapeDtypeStruct((B, S, H * dh), BF16),
        scratch_shapes=[pltpu.VMEM((S + 16, dh), F32),
                        pltpu.VMEM((S, dh), BF16),
                        pltpu.VMEM((dh, S), BF16),
                        pltpu.VMEM((S, 2 * dh), BF16),
                        pltpu.VMEM((S, dh), F32),
                        pltpu.VMEM((nc, L), F32)],
        compiler_params=_cparams(("arbitrary", "arbitrary")),
        name="mlstm",
    )(pm, pm, pm, pm, conv_w, conv_w, gt, norm_g)


def _attn_body(q_ref, k_ref, v_ref, lam_ref, sg_ref, y_ref, va_s, kt_s, s_s, *, S, tq, tk, dv, lam_init):
    nk = S // tk
    lanes = 128

    @pl.when(pl.program_id(2) == 0)
    def _():
        ones_col = (lax.broadcasted_iota(I32, (tk, dv), 1) == 0).astype(BF16)
        for kb in range(nk):
            va_s[kb * tk:(kb + 1) * tk, 0:dv] = v_ref[0, kb * tk:(kb + 1) * tk, :]
            va_s[kb * tk:(kb + 1) * tk, dv:2 * dv] = ones_col
            kt_s[:, kb * tk:(kb + 1) * tk] = k_ref[0, kb * tk:(kb + 1) * tk, :].astype(F32).T.astype(BF16)

    lp = lam_ref[...]
    lam = (jnp.exp(jnp.sum(lp[0:1, :] * lp[1:2, :], axis=1, keepdims=True))
           - jnp.exp(jnp.sum(lp[2:3, :] * lp[3:4, :], axis=1, keepdims=True)) + lam_init)

    q = q_ref[0]
    lane = lax.broadcasted_iota(I32, q.shape, 1)
    hd = q.shape[1] // 2

    qs = jnp.concatenate([jnp.where(lane < hd, q, jnp.zeros_like(q)),
                          jnp.where(lane >= hd, q, jnp.zeros_like(q))], axis=0)
    G = ATT_GROUPS
    rg = 2 * tq // G
    mparts, mbs, accs = [None] * G, [None] * G, [None] * G

    def score(g, kb):
        rows = slice(g * rg, (g + 1) * rg)
        s = jnp.dot(qs[rows, :], kt_s[:, kb * tk:(kb + 1) * tk], preferred_element_type=F32)
        s_s[rows, kb * tk:(kb + 1) * tk] = s
        for j in range(tk // lanes):
            blk = s[:, j * lanes:(j + 1) * lanes]
            mparts[g] = blk if mparts[g] is None else jnp.maximum(mparts[g], blk)

    def rowmax(g):
        mbs[g] = jnp.broadcast_to(jnp.max(mparts[g], axis=1, keepdims=True), (rg, lanes))

    def weigh(g, kb):
        rows = slice(g * rg, (g + 1) * rg)
        p = jnp.concatenate(
            [jnp.exp2(s_s[rows, kb * tk + j * lanes:kb * tk + (j + 1) * lanes] - mbs[g]).astype(BF16)
             for j in range(tk // lanes)], axis=1)
        part = jnp.dot(p, va_s[kb * tk:(kb + 1) * tk, :], preferred_element_type=F32)
        accs[g] = part if accs[g] is None else accs[g] + part

    first, second = range(G // 2), range(G // 2, G)
    for kb in range(nk):
        for g in first:
            score(g, kb)
    for g in first:
        rowmax(g)
    for kb in range(nk):
        for g in second:
            score(g, kb)
        for g in first:
            weigh(g, kb)
    for g in second:
        rowmax(g)
    for kb in range(nk):
        for g in second:
            weigh(g, kb)
    acc = jnp.concatenate(accs, axis=0)
    r = acc[:, 0:dv] * (1.0 / acc[:, dv:dv + 1])
    o = r[0:tq, :] - lam * r[tq:2 * tq, :]
    y_ref[0] = (_rms(o) * sg_ref[...] * (1.0 - lam_init)).astype(BF16)


def _attn(qn, kn, va, lam_p, subln_g, layer_idx):
    B, S, W = qn.shape
    dv = W // A_HEADS
    tq = min(ATT_TQ, S)
    tk = min(ATT_TK, S)
    lam_init = 0.8 - 0.6 * math.exp(-0.3 * layer_idx)
    body = functools.partial(_attn_body, S=S, tq=tq, tk=tk, dv=dv, lam_init=lam_init)
    return pl.pallas_call(
        body,
        grid=(B, A_HEADS, S // tq),
        in_specs=[pl.BlockSpec((1, tq, dv), lambda b, h, i: (b, i, h)),
                  pl.BlockSpec((1, S, dv), lambda b, h, i: (b, 0, h)),
                  pl.BlockSpec((1, S, dv), lambda b, h, i: (b, 0, h)),
                  pl.BlockSpec(lam_p.shape, lambda b, h, i: (0, 0)),
                  pl.BlockSpec((1, dv), lambda b, h, i: (0, 0))],
        out_specs=pl.BlockSpec((1, tq, dv), lambda b, h, i: (b, i, h)),
        out_shape=jax.ShapeDtypeStruct((B, S, W), BF16),
        scratch_shapes=[pltpu.VMEM((S, 2 * dv), BF16), pltpu.VMEM((dv, S), BF16), pltpu.VMEM((2 * tq, S), F32)],
        compiler_params=_cparams(("arbitrary", "arbitrary", "arbitrary")),
        name="attn",
    )(qn, kn, va, lam_p, subln_g.reshape(1, dv))


def _outproj_body(ym_ref, ya_ref, x_ref, mod_ref, wom_ref, woa_ref, g2_ref, wrh_ref, wrc_ref,
                  x1_ref, h2_ref, aff_ref, *, sub):
    for r in range(x_ref.shape[1] // sub):
        rows = slice(r * sub, (r + 1) * sub)
        mix = (jnp.dot(ym_ref[0, rows, :], wom_ref[...], preferred_element_type=F32)
               + jnp.dot(ya_ref[0, rows, :], woa_ref[...], preferred_element_type=F32))
        x1 = x_ref[0, rows, :] + mod_ref[0, 2:3, :] * mix
        x1_ref[0, rows, :] = x1
        h2 = _rms(x1) * g2_ref[...] * (1.0 + mod_ref[0, 4:5, :]) + mod_ref[0, 3:4, :]
        hh = h2.astype(BF16)
        h2_ref[0, rows, :] = hh
        hl = (h2 - hh.astype(F32)).astype(BF16)
        E = wrh_ref.shape[1]
        both = jnp.dot(hh, wrc_ref[...], preferred_element_type=F32)
        logits = both[:, 0:E] + both[:, E:2 * E] + jnp.dot(hl, wrh_ref[...], preferred_element_type=F32)
        ex = jnp.exp(logits - jnp.max(logits, axis=1, keepdims=True))
        aff_ref[0, rows, :] = ex * (1.0 / jnp.sum(ex, axis=1, keepdims=True))


def _outproj(ym, ya, x, mod, wom, woa, g2, wrh, wrc):
    B, S, D = x.shape
    tm = min(OUTPROJ_TILE, S)
    wm, wa, E = ym.shape[2], ya.shape[2], wrh.shape[1]
    full = lambda shape: pl.BlockSpec(shape, lambda b, i: (0, 0))
    return pl.pallas_call(
        functools.partial(_outproj_body, sub=min(OUTPROJ_SUB, tm)),
        grid=(B, S // tm),
        in_specs=[pl.BlockSpec((1, tm, wm), lambda b, i: (b, i, 0)),
                  pl.BlockSpec((1, tm, wa), lambda b, i: (b, i, 0)),
                  pl.BlockSpec((1, tm, D), lambda b, i: (b, i, 0)),
                  pl.BlockSpec((1, 6, D), lambda b, i: (b, 0, 0)),
                  full((wm, D)), full((wa, D)), full((1, D)), full((D, E)), full((D, 2 * E))],
        out_specs=[pl.BlockSpec((1, tm, D), lambda b, i: (b, i, 0)),
                   pl.BlockSpec((1, tm, D), lambda b, i: (b, i, 0)),
                   pl.BlockSpec((1, tm, E), lambda b, i: (b, i, 0))],
        out_shape=[jax.ShapeDtypeStruct((B, S, D), F32),
                   jax.ShapeDtypeStruct((B, S, D), BF16),
                   jax.ShapeDtypeStruct((B, S, E), F32)],
        compiler_params=_cparams(("arbitrary", "arbitrary")),
        name="outproj",
    )(ym, ya, x, mod, wom, woa, g2, wrh, wrc)


def _topk_body(a_ref, rank_ref, starts_ref, *, S, cap, tb):
    bits = lax.bitcast_convert_type(a_ref[0], I32)
    E = bits.shape[0]

    def search(i, thr):
        cand = thr | jnp.left_shift(jnp.int32(1), 30 - i)
        cnt = jnp.sum((bits >= cand).astype(I32), axis=1, keepdims=True)
        return jnp.where(cnt >= cap, cand, thr)
    thr = lax.fori_loop(0, 31, search, jnp.zeros((E, 1), I32))

    gt = bits > thr
    eq = bits == thr
    need = cap - jnp.sum(gt.astype(I32), axis=1, keepdims=True)

    nt = S // tb
    upper = (lax.broadcasted_iota(I32, (tb, tb), 0) < lax.broadcasted_iota(I32, (tb, tb), 1)).astype(BF16)

    def excl_cumsum(mask, with_starts):
        off = jnp.zeros((E, 1), F32)
        outs, starts = [], []
        for t in range(nt):
            blk = mask[:, t * tb:(t + 1) * tb].astype(F32).astype(BF16)
            outs.append(jnp.dot(blk, upper, preferred_element_type=F32) + off)
            starts.append(off)
            off = off + jnp.sum(blk.astype(F32), axis=1, keepdims=True)
        res = jnp.concatenate(outs, axis=1).astype(I32)
        return (res, jnp.concatenate(starts, axis=1).astype(I32)) if with_starts else res

    sel = gt | (eq & (excl_cumsum(eq, False) < need))
    rank, starts = excl_cumsum(sel, True)
    rank_ref[0] = jnp.where(sel, rank, -1)
    starts_ref[0] = starts


def _topk(aff_t, cap):
    B, E, S = aff_t.shape
    tb = min(TOK_TILE, S)
    nt = S // tb
    body = functools.partial(_topk_body, S=S, cap=cap, tb=tb)
    return pl.pallas_call(
        body,
        grid=(B,),
        in_specs=[pl.BlockSpec((1, E, S), lambda b: (b, 0, 0))],
        out_specs=[pl.BlockSpec((1, E, S), lambda b: (b, 0, 0)),
                   pl.BlockSpec((1, E, nt), lambda b: (b, 0, 0))],
        out_shape=[jax.ShapeDtypeStruct((B, E, S), I32), jax.ShapeDtypeStruct((B, E, nt), I32)],
        compiler_params=_cparams(("arbitrary",)),
        name="topk",
    )(aff_t)


def _gather_body(starts_ref, h2_ref, aff_ref, rank_ref, xe_ref, acc_s, *, S, E, EG, cap, tb):
    b = pl.program_id(0)
    g = pl.program_id(1)
    nt = S // tb
    D = h2_ref.shape[2]
    DX = acc_s.shape[2]
    big = tb + BF16_ROWS
    sm = min(SMALL_WIN, big)
    head = lax.broadcasted_iota(I32, (BF16_ROWS, DX), 0)

    def bounds(j, t):
        row = (b * E + g * EG + j) * nt
        r0 = starts_ref[row + t]
        r_end = jnp.where(t == nt - 1, cap, starts_ref[row + jnp.minimum(t + 1, nt - 1)])
        r0a = pl.multiple_of((r0 // BF16_ROWS) * BF16_ROWS, BF16_ROWS)
        return r0, r_end, r0a

    def put(j, r0, r0a, rows, hit, off, w):
        a = jnp.sum(jnp.where(hit, aff_ref[0, j, :, pl.ds(off, tb)], 0.0), axis=1, keepdims=True)
        a1 = a.astype(BF16).astype(F32)
        a2 = (a - a1).astype(BF16).astype(F32)
        a3 = a - a1 - a2
        xlane = lax.broadcasted_iota(I32, (w, AFF_LANES), 1)
        ext = jnp.where(xlane == 0, a1, jnp.where(xlane == 1, a2, jnp.where(xlane == 2, a3, 0.0))).astype(BF16)
        full = jnp.concatenate([rows, ext], axis=1)
        keep = acc_s[j, pl.ds(r0a, BF16_ROWS), :]
        acc_s[j, pl.ds(r0a, BF16_ROWS), :] = jnp.where(head < r0 - r0a, keep, full[0:BF16_ROWS, :])
        acc_s[j, pl.ds(r0a + BF16_ROWS, w - BF16_ROWS), :] = full[BF16_ROWS:, :]

    ntile = min(GATHER_TILES, nt)

    def trip(i, _):
        ts = [i * ntile + u for u in range(ntile)]
        offs = [pl.multiple_of(t * tb, tb) for t in ts]
        bnds = [[bounds(j, t) for j in range(EG)] for t in ts]
        fits = None
        for per_tile in bnds:
            for r0, r_end, r0a in per_tile:
                f = r_end - r0a < sm
                fits = f if fits is None else (fits & f)

        @pl.when(fits)
        def _():
            sub = lax.broadcasted_iota(I32, (sm, tb), 0)
            for off, bnd in zip(offs, bnds):
                hits = [sub == rank_ref[0, j, :, pl.ds(off, tb)] - bnd[j][2] for j in range(EG)]
                onehot = jnp.concatenate([h.astype(F32) for h in hits], axis=0).astype(BF16)
                rows = jnp.dot(onehot, h2_ref[0, pl.ds(off, tb), :], preferred_element_type=F32).astype(BF16)
                for j in range(EG):
                    put(j, bnd[j][0], bnd[j][2], rows[j * sm:(j + 1) * sm, :], hits[j], off, sm)

        @pl.when(jnp.logical_not(fits))
        def _():
            sub = lax.broadcasted_iota(I32, (big, tb), 0)
            for off, bnd in zip(offs, bnds):
                for j in range(EG):
                    hit = sub == rank_ref[0, j, :, pl.ds(off, tb)] - bnd[j][2]
                    rows = jnp.dot(hit.astype(F32).astype(BF16), h2_ref[0, pl.ds(off, tb), :],
                                   preferred_element_type=F32).astype(BF16)
                    put(j, bnd[j][0], bnd[j][2], rows, hit, off, big)
        return 0

    for j in range(EG):
        acc_s[j, 0:BF16_ROWS, :] = jnp.zeros((BF16_ROWS, DX), BF16)
    lax.fori_loop(0, nt // ntile, trip, 0)
    for j in range(EG):
        xe_ref[j] = acc_s[j, 0:cap, :]


def _gather(starts, h2, aff_t, rank_t, cap):
    B, S, D = h2.shape
    E = rank_t.shape[1]
    EG = min(GATHER_EXPERTS, E)
    tb = min(TOK_TILE, S)
    DX = D + AFF_LANES
    body = functools.partial(_gather_body, S=S, E=E, EG=EG, cap=cap, tb=tb)
    grid_spec = pltpu.PrefetchScalarGridSpec(
        num_scalar_prefetch=1,
        grid=(B, E // EG),
        in_specs=[pl.BlockSpec((1, S, D), lambda b, g, st: (b, 0, 0)),
                  pl.BlockSpec((1, EG, 1, S), lambda b, g, st: (b, g, 0, 0)),
                  pl.BlockSpec((1, EG, 1, S), lambda b, g, st: (b, g, 0, 0))],
        out_specs=pl.BlockSpec((EG, cap, DX), lambda b, g, st: (g, b, 0)),
        scratch_shapes=[pltpu.VMEM((EG, cap + tb + BF16_ROWS, DX), BF16)],
    )
    return pl.pallas_call(
        body,
        grid_spec=grid_spec,
        out_shape=jax.ShapeDtypeStruct((E, B * cap, DX), BF16),
        compiler_params=_cparams(("arbitrary", "arbitrary")),
        name="gather",
    )(starts.reshape(-1), h2, aff_t.reshape(B, E, 1, S), rank_t.reshape(B, E, 1, S))


def _ffn_body(x_ref, wg_ref, wu_ref, wd_ref, y_ref, acc_s, *, tt, sub, nfc):
    fc = pl.program_id(2)
    D = y_ref.shape[2]

    def mm(a, w):
        return lax.dot_general(a, w, (((1,), (0,)), ((), ())), preferred_element_type=F32)

    if nfc > 1:
        @pl.when((pl.program_id(0) == 0) & (pl.program_id(1) == 0) & (fc == 0))
        def _():
            acc_s[...] = jnp.zeros(acc_s.shape, F32)

    for i in range(tt // sub):
        rows = slice(i * sub, (i + 1) * sub)
        x = x_ref[0, rows, 0:D]
        g = mm(x, wg_ref[0])
        u = mm(x, wu_ref[0])
        a = (g * _sigmoid(g) * u).astype(BF16)
        total = mm(a, wd_ref[0])
        if nfc > 1:
            total = jnp.where(fc == 0, total, acc_s[rows, :] + total)
            acc_s[rows, :] = total
        val = jnp.sum(x_ref[0, rows, D:].astype(F32), axis=1, keepdims=True)
        y_ref[0, rows, :] = (total * val).astype(BF16)


def _ffn(xe, w_gate, w_up, w_down):
    E, T, DX = xe.shape
    D = w_gate.shape[1]
    F = w_gate.shape[2]
    tt = min(FFN_ROWS, T)
    tf = min(FFN_HIDDEN, F)
    sub = min(FFN_SUB, tt)
    body = functools.partial(_ffn_body, tt=tt, sub=sub, nfc=F // tf)
    return pl.pallas_call(
        body,
        grid=(E, T // tt, F // tf),
        in_specs=[pl.BlockSpec((1, tt, DX), lambda e, t, f: (e, t, 0)),
                  pl.BlockSpec((1, D, tf), lambda e, t, f: (e, 0, f)),
                  pl.BlockSpec((1, D, tf), lambda e, t, f: (e, 0, f)),
                  pl.BlockSpec((1, tf, D), lambda e, t, f: (e, f, 0))],
        out_specs=pl.BlockSpec((1, tt, D), lambda e, t, f: (e, t, 0)),
        out_shape=jax.ShapeDtypeStruct((E, T, D), BF16),
        scratch_shapes=[pltpu.VMEM((tt, D), F32)],
        compiler_params=_cparams(("arbitrary", "arbitrary", "arbitrary")),
        name="ffn",
    )(xe, w_gate, w_up, w_down)


def _combine_body(starts_ref, y_ref, rank_ref, x1_ref, mod_ref, o_ref, ycat_s, *, S, E, cap, tb):
    b = pl.program_id(0)
    nt = S // tb
    nsub = o_ref.shape[1] // tb
    kw = min(tb, cap)
    sm = min(SMALL_WIN, kw)
    lanes = 128
    per = lanes // sm
    gate = mod_ref[0, 5:6, :]

    def window(u, e, w):
        t = pl.program_id(1) * nsub + u
        row = (b * E + e) * nt
        r0 = starts_ref[row + t]
        r_end = jnp.where(t == nt - 1, cap, starts_ref[row + jnp.minimum(t + 1, nt - 1)])
        r0a = pl.multiple_of(jnp.minimum((r0 // BF16_ROWS) * BF16_ROWS, cap - w), BF16_ROWS)
        return r0a, r_end > r0a + w

    fits = None
    for u in range(nsub):
        for e in range(E):
            f = jnp.logical_not(window(u, e, sm)[1])
            fits = f if fits is None else (fits & f)

    @pl.when(fits)
    def _():
        lane = lax.broadcasted_iota(I32, (tb, lanes), 1)
        for u in range(nsub):
            rows = slice(u * tb, (u + 1) * tb)
            rank = rank_ref[0, rows, :]
            tiles = []
            for k in range(E // per):
                hit = None
                for i in range(per):
                    e = k * per + i
                    r0a, _ = window(u, e, sm)
                    ycat_s[u, e * sm:(e + 1) * sm, :] = y_ref[e, pl.ds(r0a, sm), :]
                    col = rank[:, e:e + 1] - r0a
                    tgt = jnp.where((col >= 0) & (col < sm), col + i * sm, -1)
                    h = lane == tgt
                    hit = h if hit is None else (hit | h)
                tiles.append(hit.astype(F32).astype(BF16))
            onehot = jnp.concatenate(tiles, axis=1)
            o_ref[0, rows, :] = x1_ref[0, rows, :] + gate * jnp.dot(onehot, ycat_s[u],
                                                                   preferred_element_type=F32)

    @pl.when(jnp.logical_not(fits))
    def _():
        lane = lax.broadcasted_iota(I32, (tb, kw), 1)
        for u in range(nsub):
            rows = slice(u * tb, (u + 1) * tb)
            rank = rank_ref[0, rows, :]
            acc = None
            for e in range(E):
                r0a, _ = window(u, e, kw)
                onehot = (rank[:, e:e + 1] - r0a == lane).astype(F32).astype(BF16)
                part = jnp.dot(onehot, y_ref[e, pl.ds(r0a, kw), :], preferred_element_type=F32)
                acc = part if acc is None else acc + part
            o_ref[0, rows, :] = x1_ref[0, rows, :] + gate * acc

            if cap > kw:
                for e in range(E):
                    r0a, over = window(u, e, kw)

                    @pl.when(over)
                    def _(e=e, r0a=r0a, rows=rows):
                        r1a = pl.multiple_of(jnp.minimum(r0a + kw, cap - kw), BF16_ROWS)
                        rk = rank_ref[0, rows, e:e + 1]
                        oh = ((rk - r1a == lane) & (rk >= r0a + kw)).astype(F32).astype(BF16)
                        o_ref[0, rows, :] += gate * jnp.dot(oh, y_ref[e, pl.ds(r1a, kw), :],
                                                            preferred_element_type=F32)


def _combine(starts, y, rank, x1, mod, cap):
    B, S, D = x1.shape
    E = rank.shape[2]
    tb = min(TOK_TILE, S)
    nsub = min(COMBINE_TILES, S // tb)
    rows = nsub * tb
    body = functools.partial(_combine_body, S=S, E=E, cap=cap, tb=tb)
    grid_spec = pltpu.PrefetchScalarGridSpec(
        num_scalar_prefetch=1,
        grid=(B, S // rows),
        in_specs=[pl.BlockSpec((E, cap, D), lambda b, t, st: (0, b, 0)),
                  pl.BlockSpec((1, rows, E), lambda b, t, st: (b, t, 0)),
                  pl.BlockSpec((1, rows, D), lambda b, t, st: (b, t, 0)),
                  pl.BlockSpec((1, 6, D), lambda b, t, st: (b, 0, 0))],
        out_specs=pl.BlockSpec((1, rows, D), lambda b, t, st: (b, t, 0)),
        scratch_shapes=[pltpu.VMEM((nsub, E * min(SMALL_WIN, tb, cap), D), BF16)],
    )
    return pl.pallas_call(
        body,
        grid_spec=grid_spec,
        out_shape=jax.ShapeDtypeStruct((B, S, D), F32),
        compiler_params=_cparams(("arbitrary", "arbitrary")),
        name="combine",
    )(starts.reshape(-1), y, rank, x1, mod)


def _layer(layer_idx, x, c, positions, norm1_g, norm2_g, w_ada, b_ada, w_in, conv_w, gate_b, mnorm_g, qk_g,
           lam_p, subln_g, w_out, w_router, w_gate_e, w_up_e, w_down_e):
    B, S, D = x.shape
    E = w_router.shape[1]
    d_m = mnorm_g.shape[0]
    n_g = 4 * M_HEADS
    d_a = w_out.shape[0] - d_m
    cap = CAP_FACTOR * S // E
    L = min(MLSTM_CHUNK, S)

    mod = _ada(c, w_ada, b_ada).reshape(B, 6, D)

    wm = w_in[:, :4 * d_m].astype(BF16)
    wg = w_in[:, 4 * d_m:4 * d_m + n_g].astype(BF16)
    wa = w_in[:, 4 * d_m + n_g:].astype(BF16)
    pm, qn, kn, va, gates = _inproj(x, mod, norm1_g.reshape(1, D), wm, wa, wg, gate_b.reshape(1, n_g),
                                    positions, qk_g)

    gt = gates.reshape(B, S // L, L, n_g).transpose(0, 3, 1, 2)
    ym = _mlstm(pm, conv_w, gt, mnorm_g.reshape(1, d_m))
    ya = _attn(qn, kn, va, lam_p, subln_g, layer_idx)

    wo = w_out.astype(BF16)
    wr_hi = w_router.astype(BF16)
    wr_lo = (w_router - wr_hi.astype(F32)).astype(BF16)
    x1, h2, aff = _outproj(ym, ya, x, mod, wo[:d_m], wo[d_m:], norm2_g.reshape(1, D), wr_hi,
                           jnp.concatenate([wr_hi, wr_lo], axis=1))

    aff_t = aff.transpose(0, 2, 1)
    rank_t, starts = _topk(aff_t, cap)
    xe = _gather(starts, h2, aff_t, rank_t, cap)
    y = _ffn(xe, w_gate_e, w_up_e, w_down_e)
    return _combine(starts, y, rank_t.transpose(0, 2, 1), x1, mod, cap)


def kernel(x, c, positions, norm1_g, norm2_g, w_ada, b_ada, w_in, mlstm_conv_w, mlstm_gate_b, mlstm_norm_g,
           diff_qk_g, diff_lambda, diff_subln_g, w_out, w_router, w_gate_e, w_up_e, w_down_e):
    for l in range(norm1_g.shape[0]):
        x = _layer(l, x, c, positions, norm1_g[l], norm2_g[l], w_ada[l], b_ada[l], w_in[l], mlstm_conv_w[l],
                   mlstm_gate_b[l], mlstm_norm_g[l], diff_qk_g[l], diff_lambda[l], diff_subln_g[l], w_out[l],
                   w_router[l], w_gate_e[l], w_up_e[l], w_down_e[l])
    return x
```

```python
import functools
import math

import numpy as np
import jax
import jax.numpy as jnp
from jax import lax
from jax.experimental import pallas as pl
from jax.experimental.pallas import tpu as pltpu

F32 = jnp.float32
BF16 = jnp.bfloat16
I32 = jnp.int32

M_HEADS = 4
A_HEADS = 4
CONV_K = 5
ROPE_THETA = 500000.0
EPS = 1e-6
CAP_FACTOR = 2

MLSTM_CHUNK = 256
SCAN_UNROLL = 8
ROW_TILE = 1024
OUTPROJ_TILE = 1024
OUTPROJ_SUB = 512
ATT_TQ = 1024
ATT_TK = 1024
ATT_GROUPS = 8
TOK_TILE = 256
BF16_ROWS = 16
AFF_LANES = 128
GATHER_TILES = 4
GATHER_EXPERTS = 4
COMBINE_TILES = 2
SMALL_WIN = 64
FFN_ROWS = 2048
FFN_HIDDEN = 512
FFN_SUB = 1024
VMEM_LIMIT = 56 * 1024 * 1024


def _cparams(sem):
    return pltpu.CompilerParams(dimension_semantics=sem, vmem_limit_bytes=VMEM_LIMIT)


def _sigmoid(v):
    return 1.0 / (1.0 + jnp.exp(-v))


def _rms(v):
    return v * lax.rsqrt(jnp.mean(v * v, axis=-1, keepdims=True) + EPS)


def _ada_body(c_ref, w_ref, b_ref, o_ref):
    c = c_ref[...]
    s = (c * _sigmoid(c)).astype(BF16)
    o_ref[...] = jnp.dot(s, w_ref[...].astype(BF16), preferred_element_type=F32) + b_ref[...]


def _ada(c, w_ada, b_ada):
    B, D = c.shape
    N = w_ada.shape[1]
    tn = D
    return pl.pallas_call(
        _ada_body,
        grid=(N // tn,),
        in_specs=[pl.BlockSpec((B, D), lambda j: (0, 0)),
                  pl.BlockSpec((D, tn), lambda j: (0, j)),
                  pl.BlockSpec((1, tn), lambda j: (0, j))],
        out_specs=pl.BlockSpec((B, tn), lambda j: (0, j)),
        out_shape=jax.ShapeDtypeStruct((B, N), F32),
        compiler_params=_cparams(("arbitrary",)),
        name="ada",
    )(c, w_ada, b_ada.reshape(1, N))


def _inproj_body(x_ref, mod_ref, g_ref, wm_ref, wa_ref, wg_ref, bg_ref, pos_ref, invf_ref, fl_ref, s1_ref, s2_ref,
                 gm_ref, gq_ref, gk_ref, pm_ref, qn_ref, kn_ref, v_ref, gt_ref, *, W, q_scale):
    s1 = s1_ref[...]
    s2 = s2_ref[...]
    gm = gm_ref[...]
    lanes = gm.shape[0]
    tm = x_ref.shape[1]

    ang_t = invf_ref[...] * pos_ref[0, 0].astype(F32)
    cs_c = jnp.cos(ang_t).T
    sn_c = jnp.sin(ang_t).T
    fl = fl_ref[...]
    cs = jnp.ones((tm, lanes), F32)
    sn = jnp.zeros((tm, lanes), F32)
    for f in range(invf_ref.shape[0]):
        cs = jnp.where(fl == f, cs_c[:, f:f + 1], cs)
        sn = jnp.where(fl == f, sn_c[:, f:f + 1], sn)

    x = x_ref[0]
    h = _rms(x) * g_ref[...]
    h = (h * (1.0 + mod_ref[0, 1:2, :]) + mod_ref[0, 0:1, :]).astype(BF16)
    pa = jnp.dot(h, wa_ref[...], preferred_element_type=F32)
    v_ref[0] = pa[:, 2 * W:3 * W].astype(BF16)

    def prep(col0, j, gain_ref, dst_ref, scale):
        t = pa[:, col0 + j * lanes:col0 + (j + 1) * lanes]
        ms = jnp.dot((t * t).astype(BF16), gm, preferred_element_type=F32)
        y = t * lax.rsqrt(ms + EPS) * gain_ref[...]
        half = pltpu.roll(y, lanes - 8, 1) * s1 + pltpu.roll(y, 8, 1) * s2
        r = y * cs + half * sn
        dst_ref[0, :, j * lanes:(j + 1) * lanes] = (r * scale).astype(BF16)

    pieces = [(0, j, gq_ref, qn_ref, q_scale) for j in range(W // lanes)] \
        + [(W, j, gk_ref, kn_ref, 1.0) for j in range(W // lanes)]
    cw = pm_ref.shape[2] // len(pieces)
    for i, piece in enumerate(pieces):
        pm_ref[0, :, i * cw:(i + 1) * cw] = jnp.dot(h, wm_ref[:, i * cw:(i + 1) * cw],
                                                     preferred_element_type=F32).astype(BF16)
        prep(*piece)
    gt_ref[0] = jnp.dot(h, wg_ref[...], preferred_element_type=F32) + bg_ref[...]


def _inproj(x, mod, g1, wm, wa, wg, bg, positions, qk_g):
    B, S, D = x.shape
    tm = min(ROW_TILE, S)
    nm, na, ng = wm.shape[1], wa.shape[1], wg.shape[1]
    W = na // 3
    hd = qk_g.shape[1]
    rot = hd // 4
    half = rot // 2
    lanes = 128
    d = np.arange(lanes) % hd
    inv_freq = (ROPE_THETA ** (-np.arange(0, rot, 2, dtype=np.float32) / rot)).astype(np.float32)
    flane = np.where(d < rot, d % half, -1).astype(np.int32)[None, :]
    s1 = np.where(d < half, -1.0, 0.0).astype(np.float32)[None, :]
    s2 = np.where((d >= half) & (d < rot), 1.0, 0.0).astype(np.float32)[None, :]
    gm = (np.arange(lanes)[:, None] // hd == np.arange(lanes)[None, :] // hd).astype(np.float32) / hd
    gq = jnp.tile(qk_g[0], lanes // hd)[None, :]
    gk = jnp.tile(qk_g[1], lanes // hd)[None, :]
    q_scale = (hd ** -0.5) * math.log2(math.e)
    body = functools.partial(_inproj_body, W=W, q_scale=q_scale)
    const = lambda shape: pl.BlockSpec(shape, lambda b, i: (0, 0))
    rows = lambda n: pl.BlockSpec((1, tm, n), lambda b, i: (b, i, 0))
    return pl.pallas_call(
        body,
        grid=(B, S // tm),
        in_specs=[rows(D),
                  pl.BlockSpec((1, 6, D), lambda b, i: (b, 0, 0)),
                  const((1, D)), const((D, nm)), const((D, na)), const((D, ng)), const((1, ng)),
                  pl.BlockSpec((1, 1, 1, tm), lambda b, i: (b, i, 0, 0)),
                  const((half, 1)), const((1, lanes)),
                  const((1, lanes)), const((1, lanes)), const((lanes, lanes)),
                  const((1, lanes)), const((1, lanes))],
        out_specs=[rows(nm), rows(W), rows(W), rows(W), rows(ng)],
        out_shape=[jax.ShapeDtypeStruct((B, S, nm), BF16),
                   jax.ShapeDtypeStruct((B, S, W), BF16),
                   jax.ShapeDtypeStruct((B, S, W), BF16),
                   jax.ShapeDtypeStruct((B, S, W), BF16),
                   jax.ShapeDtypeStruct((B, S, ng), F32)],
        compiler_params=_cparams(("arbitrary", "arbitrary")),
        name="inproj",
    )(x, mod, g1, wm, wa, wg, bg, positions.reshape(B, S // tm, 1, tm), jnp.asarray(inv_freq[:, None]),
      jnp.asarray(flane), jnp.asarray(s1), jnp.asarray(s2), jnp.asarray(gm, dtype=BF16), gq, gk)


def _log_sigmoid(v):
    return jnp.minimum(v, 0.0) - jnp.log(1.0 + jnp.exp(-jnp.abs(v)))


def _mlstm_body(q_ref, k_ref, v_ref, o_ref, cwq_ref, cwk_ref, gt_ref, ng_ref, y_ref,
                xp_s, q_s, kt_s, va_s, hf_s, a_s, *, S, L, dh):
    hd = pl.program_id(1)
    nc = S // L
    win = L + 16

    def conv(src_ref, cw_ref, post):
        xp_s[0:8, :] = jnp.zeros((8, dh), F32)
        xp_s[S + 8:S + 16, :] = jnp.zeros((8, dh), F32)

        def fill(c, _):
            base = pl.multiple_of(c * L, L)
            xp_s[pl.ds(pl.multiple_of(base + 8, 8), L), :] = src_ref[0, pl.ds(base, L), :].astype(F32)
            return 0
        lax.fori_loop(0, nc, fill, 0)

        def body(c, _):
            base = pl.multiple_of(c * L, L)
            acc = None
            for j in range(CONV_K):
                term = xp_s[pl.ds(base + (8 - CONV_K // 2 + j), L), :] * cw_ref[j:j + 1, :]
                acc = term if acc is None else acc + term
            post(base, acc * _sigmoid(acc))
            return 0
        lax.fori_loop(0, nc, body, 0, unroll=min(2, nc))

    def post_q(base, a):
        q_s[pl.ds(base, L), :] = a.astype(BF16)

    def post_k(base, a):
        kt_s[:, pl.ds(base, L)] = (a * (dh ** -0.5)).T.astype(BF16)

    conv(q_ref, cwq_ref, post_q)
    conv(k_ref, cwk_ref, post_k)

    ones_col = (lax.broadcasted_iota(I32, (L, dh), 1) == 0).astype(BF16)

    def fill_v(c, _):
        base = pl.multiple_of(c * L, L)
        va_s[pl.ds(base, L), 0:dh] = v_ref[0, pl.ds(base, L), :]
        va_s[pl.ds(base, L), dh:2 * dh] = ones_col
        return 0
    lax.fori_loop(0, nc, fill_v, 0)

    ri = lax.broadcasted_iota(I32, (L, L), 0)
    ci = lax.broadcasted_iota(I32, (L, L), 1)

    def scan(reverse, gi, gf, emit):
        logf_all = _log_sigmoid(gt_ref[0, gf])
        tri = ((ri >= ci) if reverse else (ri <= ci)).astype(F32)
        a_s[...] = jnp.dot(logf_all, tri, preferred_element_type=F32, precision=lax.Precision.HIGHEST)
        valid = (ci >= ri) if reverse else (ci <= ri)

        def body(step, carry):
            caug, m = carry
            c = (nc - 1 - step) if reverse else step
            base = pl.multiple_of(c * L, L)
            i_row = gt_ref[0, gi, pl.ds(c, 1), :]
            logf = _log_sigmoid(gt_ref[0, gf, pl.ds(c, 1), :])
            a_row = a_s[pl.ds(c, 1), :]
            a_col = jnp.sum(jnp.where(valid, logf, 0.0), axis=1, keepdims=True)
            g = jnp.sum(logf, axis=1, keepdims=True)
            dmat = jnp.where(valid, a_col - a_row + i_row, -jnp.inf)
            dmax = jnp.max(dmat, axis=1, keepdims=True)
            q_c = q_s[pl.ds(base, L), :]
            kt_c = kt_s[:, pl.ds(base, L)]
            va_c = va_s[pl.ds(base, L), :]
            qk = jnp.dot(q_c, kt_c, preferred_element_type=F32)
            p = (qk * jnp.exp(dmat - dmax)).astype(BF16)
            intra = jnp.dot(p, va_c, preferred_element_type=F32)
            inter = a_col + m
            mj = jnp.maximum(inter, dmax)
            nd = (jnp.exp(dmax - mj) * intra
                  + jnp.exp(inter - mj) * jnp.dot(q_c, caug.astype(BF16), preferred_element_type=F32))
            den = jnp.maximum(jnp.abs(nd[:, dh:dh + 1]), jnp.exp(-mj))
            emit(base, nd[:, 0:dh] * (1.0 / den))
            wlog = g - a_row + i_row
            mloc = jnp.max(wlog, axis=1, keepdims=True)
            kw = (kt_c.astype(F32) * jnp.exp(wlog - mloc)).astype(BF16)
            cloc = jnp.dot(kw, va_c, preferred_element_type=F32)
            m_new = jnp.maximum(g + m, mloc)
            caug = jnp.exp(g + m - m_new) * caug + jnp.exp(mloc - m_new) * cloc
            return caug, m_new

        lax.fori_loop(0, nc, body, (jnp.zeros((dh, 2 * dh), F32), jnp.zeros((1, 1), F32)),
                      unroll=min(SCAN_UNROLL, nc))

    def emit_f(base, h):
        hf_s[pl.ds(base, L), :] = h

    def emit_b(base, h):
        hn = _rms(hf_s[pl.ds(base, L), :] + h) * ng_ref[...]
        og = o_ref[0, pl.ds(base, L), :].astype(F32)
        y_ref[0, pl.ds(base, L), :] = (_sigmoid(og) * hn).astype(BF16)

    scan(False, hd, M_HEADS + hd, emit_f)
    scan(True, 2 * M_HEADS + hd, 3 * M_HEADS + hd, emit_b)


def _mlstm(pm, conv_w, gt, norm_g):
    B, S, _ = pm.shape
    dh = norm_g.shape[1] // M_HEADS
    L = min(MLSTM_CHUNK, S)
    nc = S // L
    H = M_HEADS
    body = functools.partial(_mlstm_body, S=S, L=L, dh=dh)
    return pl.pallas_call(
        body,
        grid=(B, H),
        in_specs=[pl.BlockSpec((1, S, dh), lambda b, h: (b, 0, h)),
                  pl.BlockSpec((1, S, dh), lambda b, h: (b, 0, H + h)),
                  pl.BlockSpec((1, S, dh), lambda b, h: (b, 0, 2 * H + h)),
                  pl.BlockSpec((1, S, dh), lambda b, h: (b, 0, 3 * H + h)),
                  pl.BlockSpec((CONV_K, dh), lambda b, h: (0, h)),
                  pl.BlockSpec((CONV_K, dh), lambda b, h: (0, H + h)),
                  pl.BlockSpec((1, 4 * H, nc, L), lambda b, h: (b, 0, 0, 0)),
                  pl.BlockSpec((1, dh), lambda b, h: (0, h))],
        out_specs=pl.BlockSpec((1, S, dh), lambda b, h: (b, 0, h)),
        out_shape=jax.ShapeDtypeStruct((B, S, H * dh), BF16),
        scratch_shapes=[pltpu.VMEM((S + 16, dh), F32),
                        pltpu.VMEM((S, dh), BF16),
                        pltpu.VMEM((dh, S), BF16),
                        pltpu.VMEM((S, 2 * dh), BF16),
                        pltpu.VMEM((S, dh), F32),
                        pltpu.VMEM((nc, L), F32)],
        compiler_params=_cparams(("arbitrary", "arbitrary")),
        name="mlstm",
    )(pm, pm, pm, pm, conv_w, conv_w, gt, norm_g)


def _attn_body(q_ref, k_ref, v_ref, lam_ref, sg_ref, y_ref, va_s, kt_s, s_s, *, S, tq, tk, dv, lam_init):
    nk = S // tk
    lanes = 128

    @pl.when(pl.program_id(2) == 0)
    def _():
        ones_col = (lax.broadcasted_iota(I32, (tk, dv), 1) == 0).astype(BF16)
        for kb in range(nk):
            va_s[kb * tk:(kb + 1) * tk, 0:dv] = v_ref[0, kb * tk:(kb + 1) * tk, :]
            va_s[kb * tk:(kb + 1) * tk, dv:2 * dv] = ones_col
            kt_s[:, kb * tk:(kb + 1) * tk] = k_ref[0, kb * tk:(kb + 1) * tk, :].astype(F32).T.astype(BF16)

    lp = lam_ref[...]
    lam = (jnp.exp(jnp.sum(lp[0:1, :] * lp[1:2, :], axis=1, keepdims=True))
           - jnp.exp(jnp.sum(lp[2:3, :] * lp[3:4, :], axis=1, keepdims=True)) + lam_init)

    q = q_ref[0]
    lane = lax.broadcasted_iota(I32, q.shape, 1)
    hd = q.shape[1] // 2

    qs = jnp.concatenate([jnp.where(lane < hd, q, jnp.zeros_like(q)),
                          jnp.where(lane >= hd, q, jnp.zeros_like(q))], axis=0)
    G = ATT_GROUPS
    rg = 2 * tq // G
    mparts, mbs, accs = [None] * G, [None] * G, [None] * G

    def score(g, kb):
        rows = slice(g * rg, (g + 1) * rg)
        s = jnp.dot(qs[rows, :], kt_s[:, kb * tk:(kb + 1) * tk], preferred_element_type=F32)
        s_s[rows, kb * tk:(kb + 1) * tk] = s
        for j in range(tk // lanes):
            blk = s[:, j * lanes:(j + 1) * lanes]
            mparts[g] = blk if mparts[g] is None else jnp.maximum(mparts[g], blk)

    def rowmax(g):
        mbs[g] = jnp.broadcast_to(jnp.max(mparts[g], axis=1, keepdims=True), (rg, lanes))

    def weigh(g, kb):
        rows = slice(g * rg, (g + 1) * rg)
        p = jnp.concatenate(
            [jnp.exp2(s_s[rows, kb * tk + j * lanes:kb * tk + (j + 1) * lanes] - mbs[g]).astype(BF16)
             for j in range(tk // lanes)], axis=1)
        part = jnp.dot(p, va_s[kb * tk:(kb + 1) * tk, :], preferred_element_type=F32)
        accs[g] = part if accs[g] is None else accs[g] + part

    first, second = range(G // 2), range(G // 2, G)
    for kb in range(nk):
        for g in first:
            score(g, kb)
    for g in first:
        rowmax(g)
    for kb in range(nk):
        for g in second:
            score(g, kb)
        for g in first:
            weigh(g, kb)
    for g in second:
        rowmax(g)
    for kb in range(nk):
        for g in second:
            weigh(g, kb)
    acc = jnp.concatenate(accs, axis=0)
    r = acc[:, 0:dv] * (1.0 / acc[:, dv:dv + 1])
    o = r[0:tq, :] - lam * r[tq:2 * tq, :]
    y_ref[0] = (_rms(o) * sg_ref[...] * (1.0 - lam_init)).astype(BF16)


def _attn(qn, kn, va, lam_p, subln_g, layer_idx):
    B, S, W = qn.shape
    dv = W // A_HEADS
    tq = min(ATT_TQ, S)
    tk = min(ATT_TK, S)
    lam_init = 0.8 - 0.6 * math.exp(-0.3 * layer_idx)
    body = functools.partial(_attn_body, S=S, tq=tq, tk=tk, dv=dv, lam_init=lam_init)
    return pl.pallas_call(
        body,
        grid=(B, A_HEADS, S // tq),
        in_specs=[pl.BlockSpec((1, tq, dv), lambda b, h, i: (b, i, h)),
                  pl.BlockSpec((1, S, dv), lambda b, h, i: (b, 0, h)),
                  pl.BlockSpec((1, S, dv), lambda b, h, i: (b, 0, h)),
                  pl.BlockSpec(lam_p.shape, lambda b, h, i: (0, 0)),
                  pl.BlockSpec((1, dv), lambda b, h, i: (0, 0))],
        out_specs=pl.BlockSpec((1, tq, dv), lambda b, h, i: (b, i, h)),
        out_shape=jax.ShapeDtypeStruct((B, S, W), BF16),
        scratch_shapes=[pltpu.VMEM((S, 2 * dv), BF16), pltpu.VMEM((dv, S), BF16), pltpu.VMEM((2 * tq, S), F32)],
        compiler_params=_cparams(("arbitrary", "arbitrary", "arbitrary")),
        name="attn",
    )(qn, kn, va, lam_p, subln_g.reshape(1, dv))


def _outproj_body(ym_ref, ya_ref, x_ref, mod_ref, wom_ref, woa_ref, g2_ref, wrh_ref, wrc_ref,
                  x1_ref, h2_ref, aff_ref, *, sub):
    for r in range(x_ref.shape[1] // sub):
        rows = slice(r * sub, (r + 1) * sub)
        mix = (jnp.dot(ym_ref[0, rows, :], wom_ref[...], preferred_element_type=F32)
               + jnp.dot(ya_ref[0, rows, :], woa_ref[...], preferred_element_type=F32))
        x1 = x_ref[0, rows, :] + mod_ref[0, 2:3, :] * mix
        x1_ref[0, rows, :] = x1
        h2 = _rms(x1) * g2_ref[...] * (1.0 + mod_ref[0, 4:5, :]) + mod_ref[0, 3:4, :]
        hh = h2.astype(BF16)
        h2_ref[0, rows, :] = hh
        hl = (h2 - hh.astype(F32)).astype(BF16)
        E = wrh_ref.shape[1]
        both = jnp.dot(hh, wrc_ref[...], preferred_element_type=F32)
        logits = both[:, 0:E] + both[:, E:2 * E] + jnp.dot(hl, wrh_ref[...], preferred_element_type=F32)
        ex = jnp.exp(logits - jnp.max(logits, axis=1, keepdims=True))
        aff_ref[0, rows, :] = ex * (1.0 / jnp.sum(ex, axis=1, keepdims=True))


def _outproj(ym, ya, x, mod, wom, woa, g2, wrh, wrc):
    B, S, D = x.shape
    tm = min(OUTPROJ_TILE, S)
    wm, wa, E = ym.shape[2], ya.shape[2], wrh.shape[1]
    full = lambda shape: pl.BlockSpec(shape, lambda b, i: (0, 0))
    return pl.pallas_call(
        functools.partial(_outproj_body, sub=min(OUTPROJ_SUB, tm)),
        grid=(B, S // tm),
        in_specs=[pl.BlockSpec((1, tm, wm), lambda b, i: (b, i, 0)),
                  pl.BlockSpec((1, tm, wa), lambda b, i: (b, i, 0)),
                  pl.BlockSpec((1, tm, D), lambda b, i: (b, i, 0)),
                  pl.BlockSpec((1, 6, D), lambda b, i: (b, 0, 0)),
                  full((wm, D)), full((wa, D)), full((1, D)), full((D, E)), full((D, 2 * E))],
        out_specs=[pl.BlockSpec((1, tm, D), lambda b, i: (b, i, 0)),
                   pl.BlockSpec((1, tm, D), lambda b, i: (b, i, 0)),
                   pl.BlockSpec((1, tm, E), lambda b, i: (b, i, 0))],
        out_shape=[jax.ShapeDtypeStruct((B, S, D), F32),
                   jax.ShapeDtypeStruct((B, S, D), BF16),
                   jax.ShapeDtypeStruct((B, S, E), F32)],
        compiler_params=_cparams(("arbitrary", "arbitrary")),
        name="outproj",
    )(ym, ya, x, mod, wom, woa, g2, wrh, wrc)


def _topk_body(a_ref, rank_ref, starts_ref, *, S, cap, tb):
    bits = lax.bitcast_convert_type(a_ref[0], I32)
    E = bits.shape[0]

    def search(i, thr):
        cand = thr | jnp.left_shift(jnp.int32(1), 30 - i)
        cnt = jnp.sum((bits >= cand).astype(I32), axis=1, keepdims=True)
        return jnp.where(cnt >= cap, cand, thr)
    thr = lax.fori_loop(0, 31, search, jnp.zeros((E, 1), I32))

    gt = bits > thr
    eq = bits == thr
    need = cap - jnp.sum(gt.astype(I32), axis=1, keepdims=True)

    nt = S // tb
    upper = (lax.broadcasted_iota(I32, (tb, tb), 0) < lax.broadcasted_iota(I32, (tb, tb), 1)).astype(BF16)

    def excl_cumsum(mask, with_starts):
        off = jnp.zeros((E, 1), F32)
        outs, starts = [], []
        for t in range(nt):
            blk = mask[:, t * tb:(t + 1) * tb].astype(F32).astype(BF16)
            outs.append(jnp.dot(blk, upper, preferred_element_type=F32) + off)
            starts.append(off)
            off = off + jnp.sum(blk.astype(F32), axis=1, keepdims=True)
        res = jnp.concatenate(outs, axis=1).astype(I32)
        return (res, jnp.concatenate(starts, axis=1).astype(I32)) if with_starts else res

    sel = gt | (eq & (excl_cumsum(eq, False) < need))
    rank, starts = excl_cumsum(sel, True)
    rank_ref[0] = jnp.where(sel, rank, -1)
    starts_ref[0] = starts


def _topk(aff_t, cap):
    B, E, S = aff_t.shape
    tb = min(TOK_TILE, S)
    nt = S // tb
    body = functools.partial(_topk_body, S=S, cap=cap, tb=tb)
    return pl.pallas_call(
        body,
        grid=(B,),
        in_specs=[pl.BlockSpec((1, E, S), lambda b: (b, 0, 0))],
        out_specs=[pl.BlockSpec((1, E, S), lambda b: (b, 0, 0)),
                   pl.BlockSpec((1, E, nt), lambda b: (b, 0, 0))],
        out_shape=[jax.ShapeDtypeStruct((B, E, S), I32), jax.ShapeDtypeStruct((B, E, nt), I32)],
        compiler_params=_cparams(("arbitrary",)),
        name="topk",
    )(aff_t)


def _gather_body(starts_ref, h2_ref, aff_ref, rank_ref, xe_ref, acc_s, *, S, E, EG, cap, tb):
    b = pl.program_id(0)
    g = pl.program_id(1)
    nt = S // tb
    D = h2_ref.shape[2]
    DX = acc_s.shape[2]
    big = tb + BF16_ROWS
    sm = min(SMALL_WIN, big)
    head = lax.broadcasted_iota(I32, (BF16_ROWS, DX), 0)

    def bounds(j, t):
        row = (b * E + g * EG + j) * nt
        r0 = starts_ref[row + t]
        r_end = jnp.where(t == nt - 1, cap, starts_ref[row + jnp.minimum(t + 1, nt - 1)])
        r0a = pl.multiple_of((r0 // BF16_ROWS) * BF16_ROWS, BF16_ROWS)
        return r0, r_end, r0a

    def put(j, r0, r0a, rows, hit, off, w):
        a = jnp.sum(jnp.where(hit, aff_ref[0, j, :, pl.ds(off, tb)], 0.0), axis=1, keepdims=True)
        a1 = a.astype(BF16).astype(F32)
        a2 = (a - a1).astype(BF16).astype(F32)
        a3 = a - a1 - a2
        xlane = lax.broadcasted_iota(I32, (w, AFF_LANES), 1)
        ext = jnp.where(xlane == 0, a1, jnp.where(xlane == 1, a2, jnp.where(xlane == 2, a3, 0.0))).astype(BF16)
        full = jnp.concatenate([rows, ext], axis=1)
        keep = acc_s[j, pl.ds(r0a, BF16_ROWS), :]
        acc_s[j, pl.ds(r0a, BF16_ROWS), :] = jnp.where(head < r0 - r0a, keep, full[0:BF16_ROWS, :])
        acc_s[j, pl.ds(r0a + BF16_ROWS, w - BF16_ROWS), :] = full[BF16_ROWS:, :]

    ntile = min(GATHER_TILES, nt)

    def trip(i, _):
        ts = [i * ntile + u for u in range(ntile)]
        offs = [pl.multiple_of(t * tb, tb) for t in ts]
        bnds = [[bounds(j, t) for j in range(EG)] for t in ts]
        fits = None
        for per_tile in bnds:
            for r0, r_end, r0a in per_tile:
                f = r_end - r0a < sm
                fits = f if fits is None else (fits & f)

        @pl.when(fits)
        def _():
            sub = lax.broadcasted_iota(I32, (sm, tb), 0)
            for off, bnd in zip(offs, bnds):
                hits = [sub == rank_ref[0, j, :, pl.ds(off, tb)] - bnd[j][2] for j in range(EG)]
                onehot = jnp.concatenate([h.astype(F32) for h in hits], axis=0).astype(BF16)
                rows = jnp.dot(onehot, h2_ref[0, pl.ds(off, tb), :], preferred_element_type=F32).astype(BF16)
                for j in range(EG):
                    put(j, bnd[j][0], bnd[j][2], rows[j * sm:(j + 1) * sm, :], hits[j], off, sm)

        @pl.when(jnp.logical_not(fits))
        def _():
            sub = lax.broadcasted_iota(I32, (big, tb), 0)
            for off, bnd in zip(offs, bnds):
                for j in range(EG):
                    hit = sub == rank_ref[0, j, :, pl.ds(off, tb)] - bnd[j][2]
                    rows = jnp.dot(hit.astype(F32).astype(BF16), h2_ref[0, pl.ds(off, tb), :],
                                   preferred_element_type=F32).astype(BF16)
                    put(j, bnd[j][0], bnd[j][2], rows, hit, off, big)
        return 0

    for j in range(EG):
        acc_s[j, 0:BF16_ROWS, :] = jnp.zeros((BF16_ROWS, DX), BF16)
    lax.fori_loop(0, nt // ntile, trip, 0)
    for j in range(EG):
        xe_ref[j] = acc_s[j, 0:cap, :]


def _gather(starts, h2, aff_t, rank_t, cap):
    B, S, D = h2.shape
    E = rank_t.shape[1]
    EG = min(GATHER_EXPERTS, E)
    tb = min(TOK_TILE, S)
    DX = D + AFF_LANES
    body = functools.partial(_gather_body, S=S, E=E, EG=EG, cap=cap, tb=tb)
    grid_spec = pltpu.PrefetchScalarGridSpec(
        num_scalar_prefetch=1,
        grid=(B, E // EG),
        in_specs=[pl.BlockSpec((1, S, D), lambda b, g, st: (b, 0, 0)),
                  pl.BlockSpec((1, EG, 1, S), lambda b, g, st: (b, g, 0, 0)),
                  pl.BlockSpec((1, EG, 1, S), lambda b, g, st: (b, g, 0, 0))],
        out_specs=pl.BlockSpec((EG, cap, DX), lambda b, g, st: (g, b, 0)),
        scratch_shapes=[pltpu.VMEM((EG, cap + tb + BF16_ROWS, DX), BF16)],
    )
    return pl.pallas_call(
        body,
        grid_spec=grid_spec,
        out_shape=jax.ShapeDtypeStruct((E, B * cap, DX), BF16),
        compiler_params=_cparams(("arbitrary", "arbitrary")),
        name="gather",
    )(starts.reshape(-1), h2, aff_t.reshape(B, E, 1, S), rank_t.reshape(B, E, 1, S))


def _ffn_body(x_ref, wg_ref, wu_ref, wd_ref, y_ref, acc_s, *, tt, sub, nfc):
    fc = pl.program_id(2)
    D = y_ref.shape[2]

    def mm(a, w):
        return lax.dot_general(a, w, (((1,), (0,)), ((), ())), preferred_element_type=F32)

    if nfc > 1:
        @pl.when((pl.program_id(0) == 0) & (pl.program_id(1) == 0) & (fc == 0))
        def _():
            acc_s[...] = jnp.zeros(acc_s.shape, F32)

    for i in range(tt // sub):
        rows = slice(i * sub, (i + 1) * sub)
        x = x_ref[0, rows, 0:D]
        g = mm(x, wg_ref[0])
        u = mm(x, wu_ref[0])
        a = (g * _sigmoid(g) * u).astype(BF16)
        total = mm(a, wd_ref[0])
        if nfc > 1:
            total = jnp.where(fc == 0, total, acc_s[rows, :] + total)
            acc_s[rows, :] = total
        val = jnp.sum(x_ref[0, rows, D:].astype(F32), axis=1, keepdims=True)
        y_ref[0, rows, :] = (total * val).astype(BF16)


def _ffn(xe, w_gate, w_up, w_down):
    E, T, DX = xe.shape
    D = w_gate.shape[1]
    F = w_gate.shape[2]
    tt = min(FFN_ROWS, T)
    tf = min(FFN_HIDDEN, F)
    sub = min(FFN_SUB, tt)
    body = functools.partial(_ffn_body, tt=tt, sub=sub, nfc=F // tf)
    return pl.pallas_call(
        body,
        grid=(E, T // tt, F // tf),
        in_specs=[pl.BlockSpec((1, tt, DX), lambda e, t, f: (e, t, 0)),
                  pl.BlockSpec((1, D, tf), lambda e, t, f: (e, 0, f)),
                  pl.BlockSpec((1, D, tf), lambda e, t, f: (e, 0, f)),
                  pl.BlockSpec((1, tf, D), lambda e, t, f: (e, f, 0))],
        out_specs=pl.BlockSpec((1, tt, D), lambda e, t, f: (e, t, 0)),
        out_shape=jax.ShapeDtypeStruct((E, T, D), BF16),
        scratch_shapes=[pltpu.VMEM((tt, D), F32)],
        compiler_params=_cparams(("arbitrary", "arbitrary", "arbitrary")),
        name="ffn",
    )(xe, w_gate, w_up, w_down)


def _combine_body(starts_ref, y_ref, rank_ref, x1_ref, mod_ref, o_ref, ycat_s, *, S, E, cap, tb):
    b = pl.program_id(0)
    nt = S // tb
    nsub = o_ref.shape[1] // tb
    kw = min(tb, cap)
    sm = min(SMALL_WIN, kw)
    lanes = 128
    per = lanes // sm
    gate = mod_ref[0, 5:6, :]

    def window(u, e, w):
        t = pl.program_id(1) * nsub + u
        row = (b * E + e) * nt
        r0 = starts_ref[row + t]
        r_end = jnp.where(t == nt - 1, cap, starts_ref[row + jnp.minimum(t + 1, nt - 1)])
        r0a = pl.multiple_of(jnp.minimum((r0 // BF16_ROWS) * BF16_ROWS, cap - w), BF16_ROWS)
        return r0a, r_end > r0a + w

    fits = None
    for u in range(nsub):
        for e in range(E):
            f = jnp.logical_not(window(u, e, sm)[1])
            fits = f if fits is None else (fits & f)

    @pl.when(fits)
    def _():
        lane = lax.broadcasted_iota(I32, (tb, lanes), 1)
        for u in range(nsub):
            rows = slice(u * tb, (u + 1) * tb)
            rank = rank_ref[0, rows, :]
            tiles = []
            for k in range(E // per):
                hit = None
                for i in range(per):
                    e = k * per + i
                    r0a, _ = window(u, e, sm)
                    ycat_s[u, e * sm:(e + 1) * sm, :] = y_ref[e, pl.ds(r0a, sm), :]
                    col = rank[:, e:e + 1] - r0a
                    tgt = jnp.where((col >= 0) & (col < sm), col + i * sm, -1)
                    h = lane == tgt
                    hit = h if hit is None else (hit | h)
                tiles.append(hit.astype(F32).astype(BF16))
            onehot = jnp.concatenate(tiles, axis=1)
            o_ref[0, rows, :] = x1_ref[0, rows, :] + gate * jnp.dot(onehot, ycat_s[u],
                                                                   preferred_element_type=F32)

    @pl.when(jnp.logical_not(fits))
    def _():
        lane = lax.broadcasted_iota(I32, (tb, kw), 1)
        for u in range(nsub):
            rows = slice(u * tb, (u + 1) * tb)
            rank = rank_ref[0, rows, :]
            acc = None
            for e in range(E):
                r0a, _ = window(u, e, kw)
                onehot = (rank[:, e:e + 1] - r0a == lane).astype(F32).astype(BF16)
                part = jnp.dot(onehot, y_ref[e, pl.ds(r0a, kw), :], preferred_element_type=F32)
                acc = part if acc is None else acc + part
            o_ref[0, rows, :] = x1_ref[0, rows, :] + gate * acc

            if cap > kw:
                for e in range(E):
                    r0a, over = window(u, e, kw)

                    @pl.when(over)
                    def _(e=e, r0a=r0a, rows=rows):
                        r1a = pl.multiple_of(jnp.minimum(r0a + kw, cap - kw), BF16_ROWS)
                        rk = rank_ref[0, rows, e:e + 1]
                        oh = ((rk - r1a == lane) & (rk >= r0a + kw)).astype(F32).astype(BF16)
                        o_ref[0, rows, :] += gate * jnp.dot(oh, y_ref[e, pl.ds(r1a, kw), :],
                                                            preferred_element_type=F32)


def _combine(starts, y, rank, x1, mod, cap):
    B, S, D = x1.shape
    E = rank.shape[2]
    tb = min(TOK_TILE, S)
    nsub = min(COMBINE_TILES, S // tb)
    rows = nsub * tb
    body = functools.partial(_combine_body, S=S, E=E, cap=cap, tb=tb)
    grid_spec = pltpu.PrefetchScalarGridSpec(
        num_scalar_prefetch=1,
        grid=(B, S // rows),
        in_specs=[pl.BlockSpec((E, cap, D), lambda b, t, st: (0, b, 0)),
                  pl.BlockSpec((1, rows, E), lambda b, t, st: (b, t, 0)),
                  pl.BlockSpec((1, rows, D), lambda b, t, st: (b, t, 0)),
                  pl.BlockSpec((1, 6, D), lambda b, t, st: (b, 0, 0))],
        out_specs=pl.BlockSpec((1, rows, D), lambda b, t, st: (b, t, 0)),
        scratch_shapes=[pltpu.VMEM((nsub, E * min(SMALL_WIN, tb, cap), D), BF16)],
    )
    return pl.pallas_call(
        body,
        grid_spec=grid_spec,
        out_shape=jax.ShapeDtypeStruct((B, S, D), F32),
        compiler_params=_cparams(("arbitrary", "arbitrary")),
        name="combine",
    )(starts.reshape(-1), y, rank, x1, mod)


def _layer(layer_idx, x, c, positions, norm1_g, norm2_g, w_ada, b_ada, w_in, conv_w, gate_b, mnorm_g, qk_g,
           lam_p, subln_g, w_out, w_router, w_gate_e, w_up_e, w_down_e):
    B, S, D = x.shape
    E = w_router.shape[1]
    d_m = mnorm_g.shape[0]
    n_g = 4 * M_HEADS
    cap = CAP_FACTOR * S // E
    L = min(MLSTM_CHUNK, S)

    mod = _ada(c, w_ada, b_ada).reshape(B, 6, D)

    wm = w_in[:, :4 * d_m].astype(BF16)
    wg = w_in[:, 4 * d_m:4 * d_m + n_g].astype(BF16)
    wa = w_in[:, 4 * d_m + n_g:].astype(BF16)
    pm, qn, kn, va, gates = _inproj(x, mod, norm1_g.reshape(1, D), wm, wa, wg, gate_b.reshape(1, n_g),
                                    positions, qk_g)

    gt = gates.reshape(B, S // L, L, n_g).transpose(0, 3, 1, 2)
    ym = _mlstm(pm, conv_w, gt, mnorm_g.reshape(1, d_m))
    ya = _attn(qn, kn, va, lam_p, subln_g, layer_idx)

    wo = w_out.astype(BF16)
    wr_hi = w_router.astype(BF16)
    wr_lo = (w_router - wr_hi.astype(F32)).astype(BF16)
    x1, h2, aff = _outproj(ym, ya, x, mod, wo[:d_m], wo[d_m:], norm2_g.reshape(1, D), wr_hi,
                           jnp.concatenate([wr_hi, wr_lo], axis=1))

    aff_t = aff.transpose(0, 2, 1)
    rank_t, starts = _topk(aff_t, cap)
    xe = _gather(starts, h2, aff_t, rank_t, cap)
    y = _ffn(xe, w_gate_e, w_up_e, w_down_e)
    return _combine(starts, y, rank_t.transpose(0, 2, 1), x1, mod, cap)


def kernel(x, c, positions, norm1_g, norm2_g, w_ada, b_ada, w_in, mlstm_conv_w, mlstm_gate_b, mlstm_norm_g,
           diff_qk_g, diff_lambda, diff_subln_g, w_out, w_router, w_gate_e, w_up_e, w_down_e):
    for l in range(norm1_g.shape[0]):
        x = _layer(l, x, c, positions, norm1_g[l], norm2_g[l], w_ada[l], b_ada[l], w_in[l], mlstm_conv_w[l],
                   mlstm_gate_b[l], mlstm_norm_g[l], diff_qk_g[l], diff_lambda[l], diff_subln_g[l], w_out[l],
                   w_router[l], w_gate_e[l], w_up_e[l], w_down_e[l])
    return x
```

```python
import functools
import math

import numpy as np
import jax
import jax.numpy as jnp
from jax import lax
from jax.experimental import pallas as pl
from jax.experimental.pallas import tpu as pltpu

F32 = jnp.float32
BF16 = jnp.bfloat16
I32 = jnp.int32

M_HEADS = 4
A_HEADS = 4
CONV_K = 5
ROPE_THETA = 500000.0
EPS = 1e-6
CAP_FACTOR = 2

MLSTM_CHUNK = 256
SCAN_UNROLL = 8
ROW_TILE = 1024
OUTPROJ_TILE = 1024
OUTPROJ_SUB = 512
ATT_TQ = 1024
ATT_TK = 1024
ATT_GROUPS = 8
TOK_TILE = 256
BF16_ROWS = 16
AFF_LANES = 128
GATHER_TILES = 4
GATHER_EXPERTS = 4
COMBINE_TILES = 2
SMALL_WIN = 64
FFN_ROWS = 2048
FFN_HIDDEN = 512
FFN_SUB = 1024
VMEM_LIMIT = 56 * 1024 * 1024


def _cparams(sem):
    return pltpu.CompilerParams(dimension_semantics=sem, vmem_limit_bytes=VMEM_LIMIT)


def _sigmoid(v):
    return 1.0 / (1.0 + jnp.exp(-v))


def _rms(v):
    return v * lax.rsqrt(jnp.mean(v * v, axis=-1, keepdims=True) + EPS)


def _ada_body(c_ref, w_ref, b_ref, o_ref):
    c = c_ref[...]
    s = (c * _sigmoid(c)).astype(BF16)
    o_ref[...] = jnp.dot(s, w_ref[...].astype(BF16), preferred_element_type=F32) + b_ref[...]


def _ada(c, w_ada, b_ada):
    B, D = c.shape
    N = w_ada.shape[1]
    tn = D
    return pl.pallas_call(
        _ada_body,
        grid=(N // tn,),
        in_specs=[pl.BlockSpec((B, D), lambda j: (0, 0)),
                  pl.BlockSpec((D, tn), lambda j: (0, j)),
                  pl.BlockSpec((1, tn), lambda j: (0, j))],
        out_specs=pl.BlockSpec((B, tn), lambda j: (0, j)),
        out_shape=jax.ShapeDtypeStruct((B, N), F32),
        compiler_params=_cparams(("arbitrary",)),
        name="ada",
    )(c, w_ada, b_ada.reshape(1, N))


def _inproj_body(x_ref, mod_ref, g_ref, wm_ref, wa_ref, wg_ref, bg_ref, pos_ref, invf_ref, fl_ref, s1_ref, s2_ref,
                 gm_ref, gq_ref, gk_ref, pm_ref, qn_ref, kn_ref, v_ref, gt_ref, *, W, q_scale):
    s1 = s1_ref[...]
    s2 = s2_ref[...]
    gm = gm_ref[...]
    lanes = gm.shape[0]
    tm = x_ref.shape[1]

    ang_t = invf_ref[...] * pos_ref[0, 0].astype(F32)
    cs_c = jnp.cos(ang_t).T
    sn_c = jnp.sin(ang_t).T
    fl = fl_ref[...]
    cs = jnp.ones((tm, lanes), F32)
    sn = jnp.zeros((tm, lanes), F32)
    for f in range(invf_ref.shape[0]):
        cs = jnp.where(fl == f, cs_c[:, f:f + 1], cs)
        sn = jnp.where(fl == f, sn_c[:, f:f + 1], sn)

    x = x_ref[0]
    h = _rms(x) * g_ref[...]
    h = (h * (1.0 + mod_ref[0, 1:2, :]) + mod_ref[0, 0:1, :]).astype(BF16)
    pa = jnp.dot(h, wa_ref[...], preferred_element_type=F32)
    v_ref[0] = pa[:, 2 * W:3 * W].astype(BF16)

    def prep(col0, j, gain_ref, dst_ref, scale):
        t = pa[:, col0 + j * lanes:col0 + (j + 1) * lanes]
        ms = jnp.dot((t * t).astype(BF16), gm, preferred_element_type=F32)
        y = t * lax.rsqrt(ms + EPS) * gain_ref[...]
        half = pltpu.roll(y, lanes - 8, 1) * s1 + pltpu.roll(y, 8, 1) * s2
        r = y * cs + half * sn
        dst_ref[0, :, j * lanes:(j + 1) * lanes] = (r * scale).astype(BF16)

    pieces = [(0, j, gq_ref, qn_ref, q_scale) for j in range(W // lanes)] \
        + [(W, j, gk_ref, kn_ref, 1.0) for j in range(W // lanes)]
    cw = pm_ref.shape[2] // len(pieces)
    for i, piece in enumerate(pieces):
        pm_ref[0, :, i * cw:(i + 1) * cw] = jnp.dot(h, wm_ref[:, i * cw:(i + 1) * cw],
                                                     preferred_element_type=F32).astype(BF16)
        prep(*piece)
    gt_ref[0] = jnp.dot(h, wg_ref[...], preferred_element_type=F32) + bg_ref[...]


def _inproj(x, mod, g1, wm, wa, wg, bg, positions, qk_g):
    B, S, D = x.shape
    tm = min(ROW_TILE, S)
    nm, na, ng = wm.shape[1], wa.shape[1], wg.shape[1]
    W = na // 3
    hd = qk_g.shape[1]
    rot = hd // 4
    half = rot // 2
    lanes = 128
    d = np.arange(lanes) % hd
    inv_freq = (ROPE_THETA ** (-np.arange(0, rot, 2, dtype=np.float32) / rot)).astype(np.float32)
    flane = np.where(d < rot, d % half, -1).astype(np.int32)[None, :]
    s1 = np.where(d < half, -1.0, 0.0).astype(np.float32)[None, :]
    s2 = np.where((d >= half) & (d < rot), 1.0, 0.0).astype(np.float32)[None, :]
    gm = (np.arange(lanes)[:, None] // hd == np.arange(lanes)[None, :] // hd).astype(np.float32) / hd
    gq = jnp.tile(qk_g[0], lanes // hd)[None, :]
    gk = jnp.tile(qk_g[1], lanes // hd)[None, :]
    q_scale = (hd ** -0.5) * math.log2(math.e)
    body = functools.partial(_inproj_body, W=W, q_scale=q_scale)
    const = lambda shape: pl.BlockSpec(shape, lambda b, i: (0, 0))
    rows = lambda n: pl.BlockSpec((1, tm, n), lambda b, i: (b, i, 0))
    return pl.pallas_call(
        body,
        grid=(B, S // tm),
        in_specs=[rows(D),
                  pl.BlockSpec((1, 6, D), lambda b, i: (b, 0, 0)),
                  const((1, D)), const((D, nm)), const((D, na)), const((D, ng)), const((1, ng)),
                  pl.BlockSpec((1, 1, 1, tm), lambda b, i: (b, i, 0, 0)),
                  const((half, 1)), const((1, lanes)),
                  const((1, lanes)), const((1, lanes)), const((lanes, lanes)),
                  const((1, lanes)), const((1, lanes))],
        out_specs=[rows(nm), rows(W), rows(W), rows(W), rows(ng)],
        out_shape=[jax.ShapeDtypeStruct((B, S, nm), BF16),
                   jax.ShapeDtypeStruct((B, S, W), BF16),
                   jax.ShapeDtypeStruct((B, S, W), BF16),
                   jax.ShapeDtypeStruct((B, S, W), BF16),
                   jax.ShapeDtypeStruct((B, S, ng), F32)],
        compiler_params=_cparams(("arbitrary", "arbitrary")),
        name="inproj",
    )(x, mod, g1, wm, wa, wg, bg, positions.reshape(B, S // tm, 1, tm), jnp.asarray(inv_freq[:, None]),
      jnp.asarray(flane), jnp.asarray(s1), jnp.asarray(s2), jnp.asarray(gm, dtype=BF16), gq, gk)


def _log_sigmoid(v):
    return jnp.minimum(v, 0.0) - jnp.log(1.0 + jnp.exp(-jnp.abs(v)))


def _mlstm_body(q_ref, k_ref, v_ref, o_ref, cwq_ref, cwk_ref, gt_ref, ng_ref, y_ref,
                xp_s, q_s, kt_s, va_s, hf_s, a_s, *, S, L, dh):
    hd = pl.program_id(1)
    nc = S // L
    win = L + 16

    def conv(src_ref, cw_ref, post):
        xp_s[0:8, :] = jnp.zeros((8, dh), F32)
        xp_s[S + 8:S + 16, :] = jnp.zeros((8, dh), F32)

        def fill(c, _):
            base = pl.multiple_of(c * L, L)
            xp_s[pl.ds(pl.multiple_of(base + 8, 8), L), :] = src_ref[0, pl.ds(base, L), :].astype(F32)
            return 0
        lax.fori_loop(0, nc, fill, 0)

        def body(c, _):
            base = pl.multiple_of(c * L, L)
            acc = None
            for j in range(CONV_K):
                term = xp_s[pl.ds(base + (8 - CONV_K // 2 + j), L), :] * cw_ref[j:j + 1, :]
                acc = term if acc is None else acc + term
            post(base, acc * _sigmoid(acc))
            return 0
        lax.fori_loop(0, nc, body, 0, unroll=min(2, nc))

    def post_q(base, a):
        q_s[pl.ds(base, L), :] = a.astype(BF16)

    def post_k(base, a):
        kt_s[:, pl.ds(base, L)] = (a * (dh ** -0.5)).T.astype(BF16)

    conv(q_ref, cwq_ref, post_q)
    conv(k_ref, cwk_ref, post_k)

    ones_col = (lax.broadcasted_iota(I32, (L, dh), 1) == 0).astype(BF16)

    def fill_v(c, _):
        base = pl.multiple_of(c * L, L)
        va_s[pl.ds(base, L), 0:dh] = v_ref[0, pl.ds(base, L), :]
        va_s[pl.ds(base, L), dh:2 * dh] = ones_col
        return 0
    lax.fori_loop(0, nc, fill_v, 0)

    ri = lax.broadcasted_iota(I32, (L, L), 0)
    ci = lax.broadcasted_iota(I32, (L, L), 1)

    def scan(reverse, gi, gf, emit):
        logf_all = _log_sigmoid(gt_ref[0, gf])
        tri = ((ri >= ci) if reverse else (ri <= ci)).astype(F32)
        a_s[...] = jnp.dot(logf_all, tri, preferred_element_type=F32, precision=lax.Precision.HIGHEST)
        valid = (ci >= ri) if reverse else (ci <= ri)

        def body(step, carry):
            caug, m = carry
            c = (nc - 1 - step) if reverse else step
            base = pl.multiple_of(c * L, L)
            i_row = gt_ref[0, gi, pl.ds(c, 1), :]
            logf = _log_sigmoid(gt_ref[0, gf, pl.ds(c, 1), :])
            a_row = a_s[pl.ds(c, 1), :]
            a_col = jnp.sum(jnp.where(valid, logf, 0.0), axis=1, keepdims=True)
            g = jnp.sum(logf, axis=1, keepdims=True)
            dmat = jnp.where(valid, a_col - a_row + i_row, -jnp.inf)
            dmax = jnp.max(dmat, axis=1, keepdims=True)
            q_c = q_s[pl.ds(base, L), :]
            kt_c = kt_s[:, pl.ds(base, L)]
            va_c = va_s[pl.ds(base, L), :]
            qk = jnp.dot(q_c, kt_c, preferred_element_type=F32)
            p = (qk * jnp.exp(dmat - dmax)).astype(BF16)
            intra = jnp.dot(p, va_c, preferred_element_type=F32)
            inter = a_col + m
            mj = jnp.maximum(inter, dmax)
            nd = (jnp.exp(dmax - mj) * intra
                  + jnp.exp(inter - mj) * jnp.dot(q_c, caug.astype(BF16), preferred_element_type=F32))
            den = jnp.maximum(jnp.abs(nd[:, dh:dh + 1]), jnp.exp(-mj))
            emit(base, nd[:, 0:dh] * (1.0 / den))
            wlog = g - a_row + i_row
            mloc = jnp.max(wlog, axis=1, keepdims=True)
            kw = (kt_c.astype(F32) * jnp.exp(wlog - mloc)).astype(BF16)
            cloc = jnp.dot(kw, va_c, preferred_element_type=F32)
            m_new = jnp.maximum(g + m, mloc)
            caug = jnp.exp(g + m - m_new) * caug + jnp.exp(mloc - m_new) * cloc
            return caug, m_new

        lax.fori_loop(0, nc, body, (jnp.zeros((dh, 2 * dh), F32), jnp.zeros((1, 1), F32)),
                      unroll=min(SCAN_UNROLL, nc))

    def emit_f(base, h):
        hf_s[pl.ds(base, L), :] = h

    def emit_b(base, h):
        hn = _rms(hf_s[pl.ds(base, L), :] + h) * ng_ref[...]
        og = o_ref[0, pl.ds(base, L), :].astype(F32)
        y_ref[0, pl.ds(base, L), :] = (_sigmoid(og) * hn).astype(BF16)

    scan(False, hd, M_HEADS + hd, emit_f)
    scan(True, 2 * M_HEADS + hd, 3 * M_HEADS + hd, emit_b)


def _mlstm(pm, conv_w, gt, norm_g):
    B, S, _ = pm.shape
    dh = norm_g.shape[1] // M_HEADS
    L = min(MLSTM_CHUNK, S)
    nc = S // L
    H = M_HEADS
    body = functools.partial(_mlstm_body, S=S, L=L, dh=dh)
    return pl.pallas_call(
        body,
        grid=(B, H),
        in_specs=[pl.BlockSpec((1, S, dh), lambda b, h: (b, 0, h)),
                  pl.BlockSpec((1, S, dh), lambda b, h: (b, 0, H + h)),
                  pl.BlockSpec((1, S, dh), lambda b, h: (b, 0, 2 * H + h)),
                  pl.BlockSpec((1, S, dh), lambda b, h: (b, 0, 3 * H + h)),
                  pl.BlockSpec((CONV_K, dh), lambda b, h: (0, h)),
                  pl.BlockSpec((CONV_K, dh), lambda b, h: (0, H + h)),
                  pl.BlockSpec((1, 4 * H, nc, L), lambda b, h: (b, 0, 0, 0)),
                  pl.BlockSpec((1, dh), lambda b, h: (0, h))],
        out_specs=pl.BlockSpec((1, S, dh), lambda b, h: (b, 0, h)),
        out_shape=jax.ShapeDtypeStruct((B, S, H * dh), BF16),
        scratch_shapes=[pltpu.VMEM((S + 16, dh), F32),
                        pltpu.VMEM((S, dh), BF16),
                        pltpu.VMEM((dh, S), BF16),
                        pltpu.VMEM((S, 2 * dh), BF16),
                        pltpu.VMEM((S, dh), F32),
                        pltpu.VMEM((nc, L), F32)],
        compiler_params=_cparams(("arbitrary", "arbitrary")),
        name="mlstm",
    )(pm, pm, pm, pm, conv_w, conv_w, gt, norm_g)


def _attn_body(q_ref, k_ref, v_ref, lam_ref, sg_ref, y_ref, va_s, kt_s, s_s, *, S, tq, tk, dv, lam_init):
    nk = S // tk
    lanes = 128

    @pl.when(pl.program_id(2) == 0)
    def _():
        ones_col = (lax.broadcasted_iota(I32, (tk, dv), 1) == 0).astype(BF16)
        for kb in range(nk):
            va_s[kb * tk:(kb + 1) * tk, 0:dv] = v_ref[0, kb * tk:(kb + 1) * tk, :]
            va_s[kb * tk:(kb + 1) * tk, dv:2 * dv] = ones_col
            kt_s[:, kb * tk:(kb + 1) * tk] = k_ref[0, kb * tk:(kb + 1) * tk, :].astype(F32).T.astype(BF16)

    lp = lam_ref[...]
    lam = (jnp.exp(jnp.sum(lp[0:1, :] * lp[1:2, :], axis=1, keepdims=True))
           - jnp.exp(jnp.sum(lp[2:3, :] * lp[3:4, :], axis=1, keepdims=True)) + lam_init)

    q = q_ref[0]
    lane = lax.broadcasted_iota(I32, q.shape, 1)
    hd = q.shape[1] // 2

    qs = jnp.concatenate([jnp.where(lane < hd, q, jnp.zeros_like(q)),
                          jnp.where(lane >= hd, q, jnp.zeros_like(q))], axis=0)
    G = ATT_GROUPS
    rg = 2 * tq // G
    mparts, mbs, accs = [None] * G, [None] * G, [None] * G

    def score(g, kb):
        rows = slice(g * rg, (g + 1) * rg)
        s = jnp.dot(qs[rows, :], kt_s[:, kb * tk:(kb + 1) * tk], preferred_element_type=F32)
        s_s[rows, kb * tk:(kb + 1) * tk] = s
        for j in range(tk // lanes):
            blk = s[:, j * lanes:(j + 1) * lanes]
            mparts[g] = blk if mparts[g] is None else jnp.maximum(mparts[g], blk)

    def rowmax(g):
        mbs[g] = jnp.broadcast_to(jnp.max(mparts[g], axis=1, keepdims=True), (rg, lanes))

    def weigh(g, kb):
        rows = slice(g * rg, (g + 1) * rg)
        p = jnp.concatenate(
            [jnp.exp2(s_s[rows, kb * tk + j * lanes:kb * tk + (j + 1) * lanes] - mbs[g]).astype(BF16)
             for j in range(tk // lanes)], axis=1)
        part = jnp.dot(p, va_s[kb * tk:(kb + 1) * tk, :], preferred_element_type=F32)
        accs[g] = part if accs[g] is None else accs[g] + part

    first, second = range(G // 2), range(G // 2, G)
    for kb in range(nk):
        for g in first:
            score(g, kb)
    for g in first:
        rowmax(g)
    for kb in range(nk):
        for g in second:
            score(g, kb)
        for g in first:
            weigh(g, kb)
    for g in second:
        rowmax(g)
    for kb in range(nk):
        for g in second:
            weigh(g, kb)
    acc = jnp.concatenate(accs, axis=0)
    r = acc[:, 0:dv] * (1.0 / acc[:, dv:dv + 1])
    o = r[0:tq, :] - lam * r[tq:2 * tq, :]
    y_ref[0] = (_rms(o) * sg_ref[...] * (1.0 - lam_init)).astype(BF16)


def _attn(qn, kn, va, lam_p, subln_g, layer_idx):
    B, S, W = qn.shape
    dv = W // A_HEADS
    tq = min(ATT_TQ, S)
    tk = min(ATT_TK, S)
    lam_init = 0.8 - 0.6 * math.exp(-0.3 * layer_idx)
    body = functools.partial(_attn_body, S=S, tq=tq, tk=tk, dv=dv, lam_init=lam_init)
    return pl.pallas_call(
        body,
        grid=(B, A_HEADS, S // tq),
        in_specs=[pl.BlockSpec((1, tq, dv), lambda b, h, i: (b, i, h)),
                  pl.BlockSpec((1, S, dv), lambda b, h, i: (b, 0, h)),
                  pl.BlockSpec((1, S, dv), lambda b, h, i: (b, 0, h)),
                  pl.BlockSpec(lam_p.shape, lambda b, h, i: (0, 0)),
                  pl.BlockSpec((1, dv), lambda b, h, i: (0, 0))],
        out_specs=pl.BlockSpec((1, tq, dv), lambda b, h, i: (b, i, h)),
        out_shape=jax.ShapeDtypeStruct((B, S, W), BF16),
        scratch_shapes=[pltpu.VMEM((S, 2 * dv), BF16), pltpu.VMEM((dv, S), BF16), pltpu.VMEM((2 * tq, S), F32)],
        compiler_params=_cparams(("arbitrary", "arbitrary", "arbitrary")),
        name="attn",
    )(qn, kn, va, lam_p, subln_g.reshape(1, dv))


def _outproj_body(ym_ref, ya_ref, x_ref, mod_ref, wom_ref, woa_ref, g2_ref, wrh_ref, wrc_ref,
                  x1_ref, h2_ref, aff_ref, *, sub):
    for r in range(x_ref.shape[1] // sub):
        rows = slice(r * sub, (r + 1) * sub)
        mix = (jnp.dot(ym_ref[0, rows, :], wom_ref[...], preferred_element_type=F32)
               + jnp.dot(ya_ref[0, rows, :], woa_ref[...], preferred_element_type=F32))
        x1 = x_ref[0, rows, :] + mod_ref[0, 2:3, :] * mix
        x1_ref[0, rows, :] = x1
        h2 = _rms(x1) * g2_ref[...] * (1.0 + mod_ref[0, 4:5, :]) + mod_ref[0, 3:4, :]
        hh = h2.astype(BF16)
        h2_ref[0, rows, :] = hh
        hl = (h2 - hh.astype(F32)).astype(BF16)
        E = wrh_ref.shape[1]
        both = jnp.dot(hh, wrc_ref[...], preferred_element_type=F32)
        logits = both[:, 0:E] + both[:, E:2 * E] + jnp.dot(hl, wrh_ref[...], preferred_element_type=F32)
        ex = jnp.exp(logits - jnp.max(logits, axis=1, keepdims=True))
        aff_ref[0, rows, :] = ex * (1.0 / jnp.sum(ex, axis=1, keepdims=True))


def _outproj(ym, ya, x, mod, wom, woa, g2, wrh, wrc):
    B, S, D = x.shape
    tm = min(OUTPROJ_TILE, S)
    wm, wa, E = ym.shape[2], ya.shape[2], wrh.shape[1]
    full = lambda shape: pl.BlockSpec(shape, lambda b, i: (0, 0))
    return pl.pallas_call(
        functools.partial(_outproj_body, sub=min(OUTPROJ_SUB, tm)),
        grid=(B, S // tm),
        in_specs=[pl.BlockSpec((1, tm, wm), lambda b, i: (b, i, 0)),
                  pl.BlockSpec((1, tm, wa), lambda b, i: (b, i, 0)),
                  pl.BlockSpec((1, tm, D), lambda b, i: (b, i, 0)),
                  pl.BlockSpec((1, 6, D), lambda b, i: (b, 0, 0)),
                  full((wm, D)), full((wa, D)), full((1, D)), full((D, E)), full((D, 2 * E))],
        out_specs=[pl.BlockSpec((1, tm, D), lambda b, i: (b, i, 0)),
                   pl.BlockSpec((1, tm, D), lambda b, i: (b, i, 0)),
                   pl.BlockSpec((1, tm, E), lambda b, i: (b, i, 0))],
        out_shape=[jax.ShapeDtypeStruct((B, S, D), F32),
                   jax.ShapeDtypeStruct((B, S, D), BF16),
                   jax.ShapeDtypeStruct((B, S, E), F32)],
        compiler_params=_cparams(("arbitrary", "arbitrary")),
        name="outproj",
    )(ym, ya, x, mod, wom, woa, g2, wrh, wrc)


def _topk_body(a_ref, rank_ref, starts_ref, *, S, cap, tb):
    bits = lax.bitcast_convert_type(a_ref[0], I32)
    E = bits.shape[0]

    def search(i, thr):
        cand = thr | jnp.left_shift(jnp.int32(1), 30 - i)
        cnt = jnp.sum((bits >= cand).astype(I32), axis=1, keepdims=True)
        return jnp.where(cnt >= cap, cand, thr)
    thr = lax.fori_loop(0, 31, search, jnp.zeros((E, 1), I32))

    gt = bits > thr
    eq = bits == thr
    need = cap - jnp.sum(gt.astype(I32), axis=1, keepdims=True)

    nt = S // tb
    upper = (lax.broadcasted_iota(I32, (tb, tb), 0) < lax.broadcasted_iota(I32, (tb, tb), 1)).astype(BF16)

    def excl_cumsum(mask, with_starts):
        off = jnp.zeros((E, 1), F32)
        outs, starts = [], []
        for t in range(nt):
            blk = mask[:, t * tb:(t + 1) * tb].astype(F32).astype(BF16)
            outs.append(jnp.dot(blk, upper, preferred_element_type=F32) + off)
            starts.append(off)
            off = off + jnp.sum(blk.astype(F32), axis=1, keepdims=True)
        res = jnp.concatenate(outs, axis=1).astype(I32)
        return (res, jnp.concatenate(starts, axis=1).astype(I32)) if with_starts else res

    sel = gt | (eq & (excl_cumsum(eq, False) < need))
    rank, starts = excl_cumsum(sel, True)
    rank_ref[0] = jnp.where(sel, rank, -1)
    starts_ref[0] = starts


def _topk(aff_t, cap):
    B, E, S = aff_t.shape
    tb = min(TOK_TILE, S)
    nt = S // tb
    body = functools.partial(_topk_body, S=S, cap=cap, tb=tb)
    return pl.pallas_call(
        body,
        grid=(B,),
        in_specs=[pl.BlockSpec((1, E, S), lambda b: (b, 0, 0))],
        out_specs=[pl.BlockSpec((1, E, S), lambda b: (b, 0, 0)),
                   pl.BlockSpec((1, E, nt), lambda b: (b, 0, 0))],
        out_shape=[jax.ShapeDtypeStruct((B, E, S), I32), jax.ShapeDtypeStruct((B, E, nt), I32)],
        compiler_params=_cparams(("arbitrary",)),
        name="topk",
    )(aff_t)


def _gather_body(starts_ref, h2_ref, aff_ref, rank_ref, xe_ref, acc_s, *, S, E, EG, cap, tb):
    b = pl.program_id(0)
    g = pl.program_id(1)
    nt = S // tb
    D = h2_ref.shape[2]
    DX = acc_s.shape[2]
    big = tb + BF16_ROWS
    sm = min(SMALL_WIN, big)
    head = lax.broadcasted_iota(I32, (BF16_ROWS, DX), 0)

    def bounds(j, t):
        row = (b * E + g * EG + j) * nt
        r0 = starts_ref[row + t]
        r_end = jnp.where(t == nt - 1, cap, starts_ref[row + jnp.minimum(t + 1, nt - 1)])
        r0a = pl.multiple_of((r0 // BF16_ROWS) * BF16_ROWS, BF16_ROWS)
        return r0, r_end, r0a

    def put(j, r0, r0a, rows, hit, off, w):
        a = jnp.sum(jnp.where(hit, aff_ref[0, j, :, pl.ds(off, tb)], 0.0), axis=1, keepdims=True)
        a1 = a.astype(BF16).astype(F32)
        a2 = (a - a1).astype(BF16).astype(F32)
        a3 = a - a1 - a2
        xlane = lax.broadcasted_iota(I32, (w, AFF_LANES), 1)
        ext = jnp.where(xlane == 0, a1, jnp.where(xlane == 1, a2, jnp.where(xlane == 2, a3, 0.0))).astype(BF16)
        full = jnp.concatenate([rows, ext], axis=1)
        keep = acc_s[j, pl.ds(r0a, BF16_ROWS), :]
        acc_s[j, pl.ds(r0a, BF16_ROWS), :] = jnp.where(head < r0 - r0a, keep, full[0:BF16_ROWS, :])
        acc_s[j, pl.ds(r0a + BF16_ROWS, w - BF16_ROWS), :] = full[BF16_ROWS:, :]

    ntile = min(GATHER_TILES, nt)

    def trip(i, _):
        ts = [i * ntile + u for u in range(ntile)]
        offs = [pl.multiple_of(t * tb, tb) for t in ts]
        bnds = [[bounds(j, t) for j in range(EG)] for t in ts]
        fits = None
        for per_tile in bnds:
            for r0, r_end, r0a in per_tile:
                f = r_end - r0a < sm
                fits = f if fits is None else (fits & f)

        @pl.when(fits)
        def _():
            sub = lax.broadcasted_iota(I32, (sm, tb), 0)
            for off, bnd in zip(offs, bnds):
                hits = [sub == rank_ref[0, j, :, pl.ds(off, tb)] - bnd[j][2] for j in range(EG)]
                onehot = jnp.concatenate([h.astype(F32) for h in hits], axis=0).astype(BF16)
                rows = jnp.dot(onehot, h2_ref[0, pl.ds(off, tb), :], preferred_element_type=F32).astype(BF16)
                for j in range(EG):
                    put(j, bnd[j][0], bnd[j][2], rows[j * sm:(j + 1) * sm, :], hits[j], off, sm)

        @pl.when(jnp.logical_not(fits))
        def _():
            sub = lax.broadcasted_iota(I32, (big, tb), 0)
            for off, bnd in zip(offs, bnds):
                for j in range(EG):
                    hit = sub == rank_ref[0, j, :, pl.ds(off, tb)] - bnd[j][2]
                    rows = jnp.dot(hit.astype(F32).astype(BF16), h2_ref[0, pl.ds(off, tb), :],
                                   preferred_element_type=F32).astype(BF16)
                    put(j, bnd[j][0], bnd[j][2], rows, hit, off, big)
        return 0

    for j in range(EG):
        acc_s[j, 0:BF16_ROWS, :] = jnp.zeros((BF16_ROWS, DX), BF16)
    lax.fori_loop(0, nt // ntile, trip, 0)
    for j in range(EG):
        xe_ref[j] = acc_s[j, 0:cap, :]


def _gather(starts, h2, aff_t, rank_t, cap):
    B, S, D = h2.shape
    E = rank_t.shape[1]
    EG = min(GATHER_EXPERTS, E)
    tb = min(TOK_TILE, S)
    DX = D + AFF_LANES
    body = functools.partial(_gather_body, S=S, E=E, EG=EG, cap=cap, tb=tb)
    grid_spec = pltpu.PrefetchScalarGridSpec(
        num_scalar_prefetch=1,
        grid=(B, E // EG),
        in_specs=[pl.BlockSpec((1, S, D), lambda b, g, st: (b, 0, 0)),
                  pl.BlockSpec((1, EG, 1, S), lambda b, g, st: (b, g, 0, 0)),
                  pl.BlockSpec((1, EG, 1, S), lambda b, g, st: (b, g, 0, 0))],
        out_specs=pl.BlockSpec((EG, cap, DX), lambda b, g, st: (g, b, 0)),
        scratch_shapes=[pltpu.VMEM((EG, cap + tb + BF16_ROWS, DX), BF16)],
    )
    return pl.pallas_call(
        body,
        grid_spec=grid_spec,
        out_shape=jax.ShapeDtypeStruct((E, B * cap, DX), BF16),
        compiler_params=_cparams(("arbitrary", "arbitrary")),
        name="gather",
    )(starts.reshape(-1), h2, aff_t.reshape(B, E, 1, S), rank_t.reshape(B, E, 1, S))


def _ffn_body(x_ref, wg_ref, wu_ref, wd_ref, y_ref, acc_s, *, tt, sub, nfc):
    fc = pl.program_id(2)
    D = y_ref.shape[2]

    def mm(a, w):
        return lax.dot_general(a, w, (((1,), (0,)), ((), ())), preferred_element_type=F32)

    if nfc > 1:
        @pl.when((pl.program_id(0) == 0) & (pl.program_id(1) == 0) & (fc == 0))
        def _():
            acc_s[...] = jnp.zeros(acc_s.shape, F32)

    for i in range(tt // sub):
        rows = slice(i * sub, (i + 1) * sub)
        x = x_ref[0, rows, 0:D]
        g = mm(x, wg_ref[0])
        u = mm(x, wu_ref[0])
        a = (g * _sigmoid(g) * u).astype(BF16)
        total = mm(a, wd_ref[0])
        if nfc > 1:
            total = jnp.where(fc == 0, total, acc_s[rows, :] + total)
            acc_s[rows, :] = total
        val = jnp.sum(x_ref[0, rows, D:].astype(F32), axis=1, keepdims=True)
        y_ref[0, rows, :] = (total * val).astype(BF16)


def _ffn(xe, w_gate, w_up, w_down):
    E, T, DX = xe.shape
    D = w_gate.shape[1]
    F = w_gate.shape[2]
    tt = min(FFN_ROWS, T)
    tf = min(FFN_HIDDEN, F)
    sub = min(FFN_SUB, tt)
    body = functools.partial(_ffn_body, tt=tt, sub=sub, nfc=F // tf)
    return pl.pallas_call(
        body,
        grid=(E, T // tt, F // tf),
        in_specs=[pl.BlockSpec((1, tt, DX), lambda e, t, f: (e, t, 0)),
                  pl.BlockSpec((1, D, tf), lambda e, t, f: (e, 0, f)),
                  pl.BlockSpec((1, D, tf), lambda e, t, f: (e, 0, f)),
                  pl.BlockSpec((1, tf, D), lambda e, t, f: (e, f, 0))],
        out_specs=pl.BlockSpec((1, tt, D), lambda e, t, f: (e, t, 0)),
        out_shape=jax.ShapeDtypeStruct((E, T, D), BF16),
        scratch_shapes=[pltpu.VMEM((tt, D), F32)],
        compiler_params=_cparams(("arbitrary", "arbitrary", "arbitrary")),
        name="ffn",
    )(xe, w_gate, w_up, w_down)


def _combine_body(starts_ref, y_ref, rank_ref, x1_ref, mod_ref, o_ref, ycat_s, *, S, E, cap, tb):
    b = pl.program_id(0)
    nt = S // tb
    nsub = o_ref.shape[1] // tb
    kw = min(tb, cap)
    sm = min(SMALL_WIN, kw)
    lanes = 128
    per = lanes // sm
    gate = mod_ref[0, 5:6, :]

    def window(u, e, w):
        t = pl.program_id(1) * nsub + u
        row = (b * E + e) * nt
        r0 = starts_ref[row + t]
        r_end = jnp.where(t == nt - 1, cap, starts_ref[row + jnp.minimum(t + 1, nt - 1)])
        r0a = pl.multiple_of(jnp.minimum((r0 // BF16_ROWS) * BF16_ROWS, cap - w), BF16_ROWS)
        return r0a, r_end > r0a + w

    fits = None
    for u in range(nsub):
        for e in range(E):
            f = jnp.logical_not(window(u, e, sm)[1])
            fits = f if fits is None else (fits & f)

    @pl.when(fits)
    def _():
        lane = lax.broadcasted_iota(I32, (tb, lanes), 1)
        for u in range(nsub):
            rows = slice(u * tb, (u + 1) * tb)
            rank = rank_ref[0, rows, :]
            tiles = []
            for k in range(E // per):
                hit = None
                for i in range(per):
                    e = k * per + i
                    r0a, _ = window(u, e, sm)
                    ycat_s[u, e * sm:(e + 1) * sm, :] = y_ref[e, pl.ds(r0a, sm), :]
                    col = rank[:, e:e + 1] - r0a
                    tgt = jnp.where((col >= 0) & (col < sm), col + i * sm, -1)
                    h = lane == tgt
                    hit = h if hit is None else (hit | h)
                tiles.append(hit.astype(F32).astype(BF16))
            onehot = jnp.concatenate(tiles, axis=1)
            o_ref[0, rows, :] = x1_ref[0, rows, :] + gate * jnp.dot(onehot, ycat_s[u],
                                                                   preferred_element_type=F32)

    @pl.when(jnp.logical_not(fits))
    def _():
        lane = lax.broadcasted_iota(I32, (tb, kw), 1)
        for u in range(nsub):
            rows = slice(u * tb, (u + 1) * tb)
            rank = rank_ref[0, rows, :]
            acc = None
            for e in range(E):
                r0a, _ = window(u, e, kw)
                onehot = (rank[:, e:e + 1] - r0a == lane).astype(F32).astype(BF16)
                part = jnp.dot(onehot, y_ref[e, pl.ds(r0a, kw), :], preferred_element_type=F32)
                acc = part if acc is None else acc + part
            o_ref[0, rows, :] = x1_ref[0, rows, :] + gate * acc

            if cap > kw:
                for e in range(E):
                    r0a, over = window(u, e, kw)

                    @pl.when(over)
                    def _(e=e, r0a=r0a, rows=rows):
                        r1a = pl.multiple_of(jnp.minimum(r0a + kw, cap - kw), BF16_ROWS)
                        rk = rank_ref[0, rows, e:e + 1]
                        oh = ((rk - r1a == lane) & (rk >= r0a + kw)).astype(F32).astype(BF16)
                        o_ref[0, rows, :] += gate * jnp.dot(oh, y_ref[e, pl.ds(r1a, kw), :],
                                                            preferred_element_type=F32)


def _combine(starts, y, rank, x1, mod, cap):
    B, S, D = x1.shape
    E = rank.shape[2]
    tb = min(TOK_TILE, S)
    nsub = min(COMBINE_TILES, S // tb)
    rows = nsub * tb
    body = functools.partial(_combine_body, S=S, E=E, cap=cap, tb=tb)
    grid_spec = pltpu.PrefetchScalarGridSpec(
        num_scalar_prefetch=1,
        grid=(B, S // rows),
        in_specs=[pl.BlockSpec((E, cap, D), lambda b, t, st: (0, b, 0)),
                  pl.BlockSpec((1, rows, E), lambda b, t, st: (b, t, 0)),
                  pl.BlockSpec((1, rows, D), lambda b, t, st: (b, t, 0)),
                  pl.BlockSpec((1, 6, D), lambda b, t, st: (b, 0, 0))],
        out_specs=pl.BlockSpec((1, rows, D), lambda b, t, st: (b, t, 0)),
        scratch_shapes=[pltpu.VMEM((nsub, E * min(SMALL_WIN, tb, cap), D), BF16)],
    )
    return pl.pallas_call(
        body,
        grid_spec=grid_spec,
        out_shape=jax.ShapeDtypeStruct((B, S, D), F32),
        compiler_params=_cparams(("arbitrary", "arbitrary")),
        name="combine",
    )(starts.reshape(-1), y, rank, x1, mod)


def _layer(layer_idx, x, c, positions, norm1_g, norm2_g, w_ada, b_ada, w_in, conv_w, gate_b, mnorm_g, qk_g,
           lam_p, subln_g, w_out, w_router, w_gate_e, w_up_e, w_down_e):
    B, S, D = x.shape
    E = w_router.shape[1]
    d_m = mnorm_g.shape[0]
    n_g = 4 * M_HEADS
    cap = CAP_FACTOR * S // E
    L = min(MLSTM_CHUNK, S)

    mod = _ada(c, w_ada, b_ada).reshape(B, 6, D)

    wm = w_in[:, :4 * d_m].astype(BF16)
    wg = w_in[:, 4 * d_m:4 * d_m + n_g].astype(BF16)
    wa = w_in[:, 4 * d_m + n_g:].astype(BF16)
    pm, qn, kn, va, gates = _inproj(x, mod, norm1_g.reshape(1, D), wm, wa, wg, gate_b.reshape(1, n_g),
                                    positions, qk_g)

    gt = gates.reshape(B, S // L, L, n_g).transpose(0, 3, 1, 2)
    ym = _mlstm(pm, conv_w, gt, mnorm_g.reshape(1, d_m))
    ya = _attn(qn, kn, va, lam_p, subln_g, layer_idx)

    wo = w_out.astype(BF16)
    wr_hi = w_router.astype(BF16)
    wr_lo = (w_router - wr_hi.astype(F32)).astype(BF16)
    x1, h2, aff = _outproj(ym, ya, x, mod, wo[:d_m], wo[d_m:], norm2_g.reshape(1, D), wr_hi,
                           jnp.concatenate([wr_hi, wr_lo], axis=1))

    aff_t = aff.transpose(0, 2, 1)
    rank_t, starts = _topk(aff_t.reshape(1, B * E, S), cap)
    rank_t, starts = rank_t.reshape(B, E, S), starts.reshape(B, E, -1)
    xe = _gather(starts, h2, aff_t, rank_t, cap)
    y = _ffn(xe, w_gate_e, w_up_e, w_down_e)
    return _combine(starts, y, rank_t.transpose(0, 2, 1), x1, mod, cap)


def kernel(x, c, positions, norm1_g, norm2_g, w_ada, b_ada, w_in, mlstm_conv_w, mlstm_gate_b, mlstm_norm_g,
           diff_qk_g, diff_lambda, diff_subln_g, w_out, w_router, w_gate_e, w_up_e, w_down_e):
    for l in range(norm1_g.shape[0]):
        x = _layer(l, x, c, positions, norm1_g[l], norm2_g[l], w_ada[l], b_ada[l], w_in[l], mlstm_conv_w[l],
                   mlstm_gate_b[l], mlstm_norm_g[l], diff_qk_g[l], diff_lambda[l], diff_subln_g[l], w_out[l],
                   w_router[l], w_gate_e[l], w_up_e[l], w_down_e[l])
    return x
```
